```python
import jax, jax.numpy as jnp
from jax import lax
import numpy as np

D_MODEL = 1024
BATCH = 4
SEQ = 4096
DEPTH = 4

HEAD_DIM = 64
FOX_HEADS = 8
SB_HEADS = 4
POOL_GROUPS = 4
POOL_WINDOWS = (2, 4, 8, 16)
POOL_GROUP_DIM = 64
FOX_W = FOX_HEADS * HEAD_DIM
SB_W = SB_HEADS * HEAD_DIM
POOL_W = POOL_GROUPS * POOL_GROUP_DIM
D_MIX = FOX_W + POOL_W + SB_W
Q_BLOCK = 128
EPS = 1e-6
NEG = -1e30

IN_SPLITS = (
    FOX_W, FOX_W, FOX_W, FOX_W,
    FOX_HEADS,
    POOL_W, POOL_W,
    SB_W, SB_W, SB_W, SB_W,
)
D_IN = sum(IN_SPLITS)

kernel_name = "hybrid_fox_pool_stickbreak_parallel_heads"


def rms_norm(x, g):
    xf = x.astype(jnp.float32)
    y = xf * lax.rsqrt(jnp.mean(xf * xf, axis=-1, keepdims=True) + EPS)
    return (y * g.astype(jnp.float32)).astype(x.dtype)


def to_heads(t, n_heads):
    b, s, _ = t.shape
    return t.reshape(b, s, n_heads, HEAD_DIM).transpose(0, 2, 1, 3)


def from_heads(t):
    b, h, s, d = t.shape
    return t.transpose(0, 2, 1, 3).reshape(b, s, h * d)


def split_blocks(t):
    b, h, s = t.shape[:3]
    nb = s // Q_BLOCK
    t = t.reshape((b, h, nb, Q_BLOCK) + t.shape[3:])
    return jnp.moveaxis(t, 2, 0)


def merge_blocks(o):
    nb, b, h, qb, d = o.shape
    return jnp.moveaxis(o, 0, 2).reshape(b, h, nb * qb, d)


def forgetting_attention(q, k, v, log_f):
    s_len, d = q.shape[2], q.shape[3]
    c = jnp.cumsum(log_f, axis=-1)
    kpos = jnp.arange(s_len)
    scale = d ** -0.5

    def one_block(args):
        qi, ci, i = args
        qpos = i * Q_BLOCK + jnp.arange(Q_BLOCK)
        sc = jnp.einsum('bhqd,bhkd->bhqk', qi, k).astype(jnp.float32) * scale
        sc = sc + (ci[..., :, None] - c[..., None, :])
        sc = jnp.where(kpos[None, :] <= qpos[:, None], sc, NEG)
        p = jax.nn.softmax(sc, axis=-1)
        return jnp.einsum('bhqk,bhkd->bhqd', p.astype(v.dtype), v)

    nb = s_len // Q_BLOCK
    o = lax.map(one_block, (split_blocks(q), split_blocks(c), jnp.arange(nb)))
    return merge_blocks(o)


def stick_breaking_attention(q, k, v):
    s_len, d = q.shape[2], q.shape[3]
    kpos = jnp.arange(s_len)
    scale = d ** -0.5

    def one_block(args):
        qi, i = args
        qpos = i * Q_BLOCK + jnp.arange(Q_BLOCK)
        z = jnp.einsum('bhqd,bhkd->bhqk', qi, k).astype(jnp.float32) * scale
        causal = kpos[None, :] < qpos[:, None]
        log_1m_beta = jnp.where(causal, -jax.nn.softplus(z), 0.0)
        rest = lax.cumsum(log_1m_beta, axis=3, reverse=True) - log_1m_beta
        a = jnp.where(causal, jnp.exp(jax.nn.log_sigmoid(z) + rest), 0.0)
        return jnp.einsum('bhqk,bhkd->bhqd', a.astype(v.dtype), v)

    nb = s_len // Q_BLOCK
    o = lax.map(one_block, (split_blocks(q), jnp.arange(nb)))
    return merge_blocks(o)


def causal_window_mean(x, w):
    s_len = x.shape[1]
    xf = x.astype(jnp.float32)
    cs = jnp.cumsum(xf, axis=1)
    cs_prev = jnp.pad(cs, ((0, 0), (w, 0), (0, 0)))[:, :s_len]
    count = jnp.minimum(jnp.arange(s_len) + 1, w).astype(jnp.float32)
    return ((cs - cs_prev) / count[None, :, None]).astype(x.dtype)


def multiscale_pool(x, w_pool, scale):
    b, s_len, _ = x.shape
    groups = jnp.split(x, POOL_GROUPS, axis=-1)
    pooled = jnp.stack([causal_window_mean(g, w) - g for g, w in zip(groups, POOL_WINDOWS)], axis=2)
    y = jnp.einsum('bsgc,gcd->bsgd', pooled, w_pool).reshape(b, s_len, POOL_W)
    return y * scale


def hybrid_layer(x, norm_g, w_in, b_f, q_norm_g, k_norm_g, w_pool, pool_scale, w_out):
    h = rms_norm(x, norm_g)
    proj = jnp.einsum('bsd,de->bse', h, w_in)
    idx = np.cumsum(IN_SPLITS)[:-1].tolist()
    (fq, fk, fv, fg, ff, px, pg, sq, sk, sv, sg) = jnp.split(proj, idx, axis=-1)

    fq = rms_norm(to_heads(fq, FOX_HEADS), q_norm_g)
    fk = rms_norm(to_heads(fk, FOX_HEADS), k_norm_g)
    fv = to_heads(fv, FOX_HEADS)
    log_f = jax.nn.log_sigmoid((ff + b_f).astype(jnp.float32)).transpose(0, 2, 1)
    fox_out = from_heads(forgetting_attention(fq, fk, fv, log_f)) * jax.nn.silu(fg)

    pool_out = multiscale_pool(px, w_pool, pool_scale) * jax.nn.silu(pg)

    sb = stick_breaking_attention(to_heads(sq, SB_HEADS), to_heads(sk, SB_HEADS), to_heads(sv, SB_HEADS))
    sb_out = from_heads(sb) * jax.nn.silu(sg)

    mixed = jnp.concatenate([fox_out, pool_out, sb_out], axis=-1)
    return x + jnp.einsum('bse,ed->bsd', mixed, w_out)


def setup_inputs(seed: int = 0) -> dict:
    key = jax.random.key(seed)
    ks = jax.random.split(key, 10)
    x = jax.random.normal(ks[0], (BATCH, SEQ, D_MODEL), jnp.float32)
    norm_g = 1.0 + 0.02 * jax.random.normal(ks[1], (DEPTH, D_MODEL), jnp.float32)
    w_in = jax.random.normal(ks[2], (DEPTH, D_MODEL, D_IN), jnp.float32) * D_MODEL ** -0.5
    b_f = jax.random.uniform(ks[3], (DEPTH, FOX_HEADS), jnp.float32, 1.0, 4.0)
    q_norm_g = 1.0 + 0.02 * jax.random.normal(ks[4], (DEPTH, HEAD_DIM), jnp.float32)
    k_norm_g = 1.0 + 0.02 * jax.random.normal(ks[5], (DEPTH, HEAD_DIM), jnp.float32)
    w_pool = jax.random.normal(ks[6], (DEPTH, POOL_GROUPS, POOL_GROUP_DIM, POOL_GROUP_DIM), jnp.float32) * POOL_GROUP_DIM ** -0.5
    pool_scale = 1.0 + 0.02 * jax.random.normal(ks[7], (DEPTH, POOL_W), jnp.float32)
    w_out = jax.random.normal(ks[8], (DEPTH, D_MIX, D_MODEL), jnp.float32) * D_MIX ** -0.5
    return {"x": x, "norm_g": norm_g, "w_in": w_in, "b_f": b_f, "q_norm_g": q_norm_g,
            "k_norm_g": k_norm_g, "w_pool": w_pool, "pool_scale": pool_scale, "w_out": w_out}


def reference(x, norm_g, w_in, b_f, q_norm_g, k_norm_g, w_pool, pool_scale, w_out):
    for l in range(DEPTH):
        x = hybrid_layer(x, norm_g[l], w_in[l], b_f[l], q_norm_g[l], k_norm_g[l],
                         w_pool[l], pool_scale[l], w_out[l])
    return x
```

```python
import functools

import jax
import jax.numpy as jnp
from jax import lax
from jax.experimental import pallas as pl
from jax.experimental.pallas import tpu as pltpu

D_MODEL = 1024
HEAD_DIM = 64
FOX_HEADS = 8
SB_HEADS = 4
POOL_GROUPS = 4
POOL_WINDOWS = (2, 4, 8, 16)
POOL_GROUP_DIM = 64
FOX_W = FOX_HEADS * HEAD_DIM
SB_W = SB_HEADS * HEAD_DIM
POOL_W = POOL_GROUPS * POOL_GROUP_DIM
D_MIX = FOX_W + POOL_W + SB_W
EPS = 1e-6
NEG = -1e30
SCALE = HEAD_DIM ** -0.5

LANES = 128
PAIR_W = 2 * HEAD_DIM
MAX_WINDOW = max(POOL_WINDOWS)
FF_PAD = LANES

ROW_BLOCK = 512
ATT_BLOCK = 256
SCAN_CHUNK = LANES

VMEM_LIMIT = 56 * 1024 * 1024

C_FQ, C_FK, C_FV, C_FG = 0, FOX_W, 2 * FOX_W, 3 * FOX_W
C_PX = 4 * FOX_W
C_PG = C_PX + POOL_W
C_SQ = C_PG + POOL_W
C_SK = C_SQ + SB_W
C_SV = C_SK + SB_W
C_SG = C_SV + SB_W
C_FF = C_SG + SB_W
D_IN_PAD = C_FF + FF_PAD


def _bf16_split2(x):
    hi = x.astype(jnp.bfloat16)
    lo = (x - hi.astype(jnp.float32)).astype(jnp.bfloat16)
    return hi, lo


def _bf16_split3(x):
    a = x.astype(jnp.bfloat16)
    r = x - a.astype(jnp.float32)
    b = r.astype(jnp.bfloat16)
    c = (r - b.astype(jnp.float32)).astype(jnp.bfloat16)
    return a, b, c


def _dot(a, b):
    return jnp.dot(a, b, preferred_element_type=jnp.float32)


def _dot_nt(a, b):
    return lax.dot_general(a, b, (((1,), (1,)), ((), ())),
                           preferred_element_type=jnp.float32)


def _silu(x):
    return x * (1.0 / (1.0 + jnp.exp(-x)))


def _softplus(x):
    return jnp.maximum(x, 0.0) + jnp.log(1.0 + jnp.exp(-jnp.abs(x)))


def _in_proj_kernel(x_ref, g_ref, w_ref, hsum_ref, qg_ref, kg_ref, bf_ref,
                    fq_ref, fk_ref, fv_ref, fg_ref, px_ref, pg_ref,
                    sq_ref, sk_ref, sv_ref, sg_ref, lf_ref):
    x = x_ref[...]
    ms = jnp.mean(x * x, axis=-1, keepdims=True)
    h = (x * lax.rsqrt(ms + EPS) * g_ref[...]).astype(jnp.bfloat16)

    def proj(c0, width):
        return _dot(h, w_ref[:, c0:c0 + width])

    def head_norm(y, gain):
        hi, lo = _bf16_split2(y * y)
        ssq = _dot(hi, hsum_ref[...]) + _dot(lo, hsum_ref[...])
        return y * lax.rsqrt(ssq * (1.0 / HEAD_DIM) + EPS) * gain

    fq_ref[...] = (head_norm(proj(C_FQ, FOX_W), qg_ref[...]) * SCALE).astype(jnp.bfloat16)
    fk_ref[...] = head_norm(proj(C_FK, FOX_W), kg_ref[...]).astype(jnp.bfloat16)
    fv_ref[...] = proj(C_FV, FOX_W).astype(jnp.bfloat16)
    fg_ref[...] = proj(C_FG, FOX_W)
    px_ref[...] = proj(C_PX, POOL_W)
    pg_ref[...] = proj(C_PG, POOL_W)
    sq_ref[...] = (proj(C_SQ, SB_W) * SCALE).astype(jnp.bfloat16)
    sk_ref[...] = proj(C_SK, SB_W).astype(jnp.bfloat16)
    sv_ref[...] = proj(C_SV, SB_W).astype(jnp.bfloat16)
    sg_ref[...] = proj(C_SG, SB_W)
    ff = proj(C_FF, FF_PAD)[:, :FOX_HEADS] + bf_ref[...]
    lf_ref[...] = -_softplus(-ff)


def _in_proj(x2d, norm_g, w_all, hsum, q_g, k_g, b_f):
    m = x2d.shape[0]
    bm = ROW_BLOCK
    row = lambda w: pl.BlockSpec((bm, w), lambda i: (i, 0))
    full = lambda a: pl.BlockSpec(a.shape, lambda i: (0,) * a.ndim)
    bf16, f32 = jnp.bfloat16, jnp.float32
    outs = [(FOX_W, bf16), (FOX_W, bf16), (FOX_W, bf16), (FOX_W, f32),
            (POOL_W, f32), (POOL_W, f32),
            (SB_W, bf16), (SB_W, bf16), (SB_W, bf16), (SB_W, f32),
            (FOX_HEADS, f32)]
    return pl.pallas_call(
        _in_proj_kernel,
        grid=(m // bm,),
        in_specs=[row(D_MODEL), full(norm_g), full(w_all), full(hsum),
                  full(q_g), full(k_g), full(b_f)],
        out_specs=[row(w) for w, _ in outs],
        out_shape=[jax.ShapeDtypeStruct((m, w), dt) for w, dt in outs],
        compiler_params=pltpu.CompilerParams(
            dimension_semantics=("arbitrary",), vmem_limit_bytes=VMEM_LIMIT),
    )(x2d, norm_g, w_all, hsum, q_g, k_g, b_f)


def _scan_kernel(lf_ref, tri_ref, prev_ref, c_ref):
    lf = lf_ref[0]
    tri = tri_ref[...]
    within = sum(_dot(p, tri) for p in _bf16_split3(lf))
    total = jnp.broadcast_to(within[:, SCAN_CHUNK - 1:SCAN_CHUNK], within.shape)
    prev = prev_ref[...]
    offset = sum(_dot(prev, p) for p in _bf16_split3(total))
    c_ref[0] = within + offset


def _forget_scan(lf_rows, tri, prev):
    b, r, c = lf_rows.shape
    full = lambda a: pl.BlockSpec(a.shape, lambda i: (0,) * a.ndim)
    blk = pl.BlockSpec((1, r, c), lambda i: (i, 0, 0))
    return pl.pallas_call(
        _scan_kernel,
        grid=(b,),
        in_specs=[blk, full(tri), full(prev)],
        out_specs=blk,
        out_shape=jax.ShapeDtypeStruct((b, r, c), jnp.float32),
        compiler_params=pltpu.CompilerParams(dimension_semantics=("arbitrary",)),
    )(lf_rows, tri, prev)


def _lane_is_first_head(shape):
    return lax.broadcasted_iota(jnp.int32, shape, len(shape) - 1) < HEAD_DIM


def _split_heads(q):
    first = _lane_is_first_head(q.shape)
    zero = jnp.zeros_like(q)
    return jnp.where(first, q, zero), jnp.where(first, zero, q)


def _widen(col, width):
    return jnp.concatenate([col] * (width // LANES), axis=1)


def _fox_kernel(q_ref, k_ref, v_ref, ccol_ref, crow_ref, g_ref, o_ref,
                m_ref, l_ref, acc_ref, ct_ref):
    t = ATT_BLOCK
    qi = pl.program_id(2)
    qs = _split_heads(q_ref[0])
    for h in range(2):
        m_ref[h] = jnp.full((t, LANES), NEG, jnp.float32)
        l_ref[h] = jnp.zeros((t, LANES), jnp.float32)
        acc_ref[h] = jnp.zeros((t, PAIR_W), jnp.float32)
        ct_ref[h] = jnp.broadcast_to(ccol_ref[0, 0, :, h:h + 1], (t, LANES))

    def step(j, masked):
        start = pl.multiple_of(j * t, t)
        k = k_ref[0, pl.ds(start, t), :]
        v = v_ref[0, pl.ds(start, t), :]
        if masked:
            rows = lax.broadcasted_iota(jnp.int32, (t, t), 0)
            cols = lax.broadcasted_iota(jnp.int32, (t, t), 1)
            keep = cols <= rows
        for h in range(2):
            cs = crow_ref[0, 0, h:h + 1, pl.ds(start, t)]
            s = _dot_nt(qs[h], k) + (_widen(ct_ref[h], t) - cs)
            if masked:
                s = jnp.where(keep, s, NEG)
            m_prev = m_ref[h]
            m_next = jnp.maximum(m_prev, jnp.max(s, axis=1, keepdims=True))
            p = jnp.exp(s - _widen(m_next, t))
            alpha = jnp.exp(m_prev - m_next)
            l_ref[h] = alpha * l_ref[h] + jnp.sum(p, axis=1, keepdims=True)
            acc_ref[h] = alpha * acc_ref[h] + _dot(p.astype(jnp.bfloat16), v)
            m_ref[h] = m_next

    def body(j, carry):
        step(j, False)
        return carry

    lax.fori_loop(0, qi, body, 0)
    step(qi, True)

    o0 = acc_ref[0] / l_ref[0]
    o1 = acc_ref[1] / l_ref[1]
    o = jnp.where(_lane_is_first_head(o0.shape), o0, o1)
    o_ref[0] = (o * _silu(g_ref[0])).astype(o_ref.dtype)


def _fox_attention(q, k, v, ccol, crow, gate):
    b, s, w = q.shape
    t = ATT_BLOCK
    pairs = w // PAIR_W
    qblk = pl.BlockSpec((1, t, PAIR_W), lambda bi, hp, qi: (bi, qi, hp))
    seq = pl.BlockSpec((1, s, PAIR_W), lambda bi, hp, qi: (bi, 0, hp))
    return pl.pallas_call(
        _fox_kernel,
        grid=(b, pairs, s // t),
        in_specs=[qblk, seq, seq,
                  pl.BlockSpec((1, 1, t, 2), lambda bi, hp, qi: (bi, hp, qi, 0)),
                  pl.BlockSpec((1, 1, 2, s), lambda bi, hp, qi: (bi, hp, 0, 0)),
                  qblk],
        out_specs=qblk,
        out_shape=jax.ShapeDtypeStruct((b, s, w), jnp.bfloat16),
        scratch_shapes=[pltpu.VMEM((2, t, LANES), jnp.float32),
                        pltpu.VMEM((2, t, LANES), jnp.float32),
                        pltpu.VMEM((2, t, PAIR_W), jnp.float32),
                        pltpu.VMEM((2, t, LANES), jnp.float32)],
        compiler_params=pltpu.CompilerParams(
            dimension_semantics=("arbitrary", "arbitrary", "arbitrary"),
            vmem_limit_bytes=VMEM_LIMIT),
    )(q, k, v, ccol, crow, gate)


def _sb_kernel(q_ref, k_ref, v_ref, g_ref, tri_ref, o_ref, r_ref, acc_ref):
    t = ATT_BLOCK
    qi = pl.program_id(2)
    qs = _split_heads(q_ref[0])
    tri = tri_ref[...]
    for h in range(2):
        r_ref[h] = jnp.zeros((t, LANES), jnp.float32)
        acc_ref[h] = jnp.zeros((t, PAIR_W), jnp.float32)

    def step(j, masked):
        start = pl.multiple_of(j * t, t)
        k = k_ref[0, pl.ds(start, t), :]
        v = v_ref[0, pl.ds(start, t), :]
        if masked:
            rows = lax.broadcasted_iota(jnp.int32, (t, t), 0)
            cols = lax.broadcasted_iota(jnp.int32, (t, t), 1)
            keep = cols < rows
        for h in range(2):
            z = _dot_nt(qs[h], k)
            sp = _softplus(z)
            if masked:
                sp = jnp.where(keep, sp, 0.0)
            hi, lo = _bf16_split2(sp)
            cum = _dot(hi, tri) + _dot(lo, tri)
            r_prev = r_ref[h]
            a = jnp.exp(z - cum - _widen(r_prev, t))
            if masked:
                a = jnp.where(keep, a, 0.0)
            acc_ref[h] = acc_ref[h] + _dot(a.astype(jnp.bfloat16), v)
            r_ref[h] = r_prev + jnp.broadcast_to(cum[:, 0:1], (t, LANES))

    step(qi, True)

    def body(i, carry):
        step(qi - 1 - i, False)
        return carry

    lax.fori_loop(0, qi, body, 0)

    o = jnp.where(_lane_is_first_head((t, PAIR_W)), acc_ref[0], acc_ref[1])
    o_ref[0] = (o * _silu(g_ref[0])).astype(o_ref.dtype)


def _sb_attention(q, k, v, gate, tri):
    b, s, w = q.shape
    t = ATT_BLOCK
    pairs = w // PAIR_W
    qblk = pl.BlockSpec((1, t, PAIR_W), lambda bi, hp, qi: (bi, qi, hp))
    seq = pl.BlockSpec((1, s, PAIR_W), lambda bi, hp, qi: (bi, 0, hp))
    return pl.pallas_call(
        _sb_kernel,
        grid=(b, pairs, s // t),
        in_specs=[qblk, seq, seq, qblk,
                  pl.BlockSpec(tri.shape, lambda bi, hp, qi: (0, 0))],
        out_specs=qblk,
        out_shape=jax.ShapeDtypeStruct((b, s, w), jnp.bfloat16),
        scratch_shapes=[pltpu.VMEM((2, t, LANES), jnp.float32),
                        pltpu.VMEM((2, t, PAIR_W), jnp.float32)],
        compiler_params=pltpu.CompilerParams(
            dimension_semantics=("arbitrary", "arbitrary", "arbitrary"),
            vmem_limit_bytes=VMEM_LIMIT),
    )(q, k, v, gate, tri)


def _out_proj_kernel(x_ref, fox_ref, sb_ref, px_ref, hist_ref, pg_ref,
                     wpool_ref, pscale_ref, wo_ref, o_ref, xp_ref, *, blocks_per_seq):
    bm = ROW_BLOCK
    i = pl.program_id(0)
    pos0 = (i % blocks_per_seq) * bm
    hist = hist_ref[...]
    xp_ref[0:MAX_WINDOW, :] = jnp.where(pos0 == 0, jnp.zeros_like(hist), hist)
    px = px_ref[...]
    xp_ref[MAX_WINDOW:MAX_WINDOW + bm, :] = px

    sums = {}
    run = px
    for d in range(1, MAX_WINDOW):
        run = run + xp_ref[MAX_WINDOW - d:MAX_WINDOW - d + bm, :]
        if d + 1 in POOL_WINDOWS:
            sums[d + 1] = run
    group = lax.broadcasted_iota(jnp.int32, (bm, POOL_W), 1) // POOL_GROUP_DIM
    pos = pos0 + lax.broadcasted_iota(jnp.int32, (bm, POOL_W), 0)
    wsum = sums[POOL_WINDOWS[-1]]
    win = jnp.full((bm, POOL_W), POOL_WINDOWS[-1], jnp.int32)
    for g in range(POOL_GROUPS - 2, -1, -1):
        wsum = jnp.where(group == g, sums[POOL_WINDOWS[g]], wsum)
        win = jnp.where(group == g, POOL_WINDOWS[g], win)
    count = jnp.minimum(pos + 1, win).astype(jnp.float32)
    pooled = wsum / count - px
    y = _dot(pooled.astype(jnp.bfloat16), wpool_ref[...]) * pscale_ref[...]
    pool_out = (y * _silu(pg_ref[...])).astype(jnp.bfloat16)

    o_ref[...] = (x_ref[...]
                  + _dot(fox_ref[...], wo_ref[0:FOX_W, :])
                  + _dot(pool_out, wo_ref[FOX_W:FOX_W + POOL_W, :])
                  + _dot(sb_ref[...], wo_ref[FOX_W + POOL_W:D_MIX, :]))


def _out_proj(x2d, fox_o, sb_o, px, pg, wpool_bd, pscale, w_out, seq_len):
    m = x2d.shape[0]
    bm = ROW_BLOCK
    row = lambda w: pl.BlockSpec((bm, w), lambda i: (i, 0))
    full = lambda a: pl.BlockSpec(a.shape, lambda i: (0,) * a.ndim)
    hist_per_block = bm // MAX_WINDOW
    hist = pl.BlockSpec((MAX_WINDOW, POOL_W),
                        lambda i: (jnp.maximum(i * hist_per_block - 1, 0), 0))
    kern = functools.partial(_out_proj_kernel, blocks_per_seq=seq_len // bm)
    return pl.pallas_call(
        kern,
        grid=(m // bm,),
        in_specs=[row(D_MODEL), row(FOX_W), row(SB_W), row(POOL_W), hist, row(POOL_W),
                  full(wpool_bd), full(pscale), full(w_out)],
        out_specs=row(D_MODEL),
        out_shape=jax.ShapeDtypeStruct((m, D_MODEL), jnp.float32),
        scratch_shapes=[pltpu.VMEM((MAX_WINDOW + bm, POOL_W), jnp.float32)],
        compiler_params=pltpu.CompilerParams(
            dimension_semantics=("arbitrary",), vmem_limit_bytes=VMEM_LIMIT),
    )(x2d, fox_o, sb_o, px, px, pg, wpool_bd, pscale, w_out)


def _constants(seq_len):
    bf16 = jnp.bfloat16
    idx = jnp.arange(FOX_W)
    hsum = (idx[:, None] // HEAD_DIM == idx[None, :] // HEAD_DIM).astype(bf16)
    c = jnp.arange(SCAN_CHUNK)
    scan_tri = (c[:, None] <= c[None, :]).astype(bf16)
    nc = seq_len // SCAN_CHUNK
    r = jnp.arange(FOX_HEADS * nc)
    scan_prev = ((r[:, None] // nc == r[None, :] // nc)
                 & (r[None, :] < r[:, None])).astype(bf16)
    a = jnp.arange(ATT_BLOCK)
    sb_tri = (a[:, None] >= a[None, :]).astype(bf16)
    return hsum, scan_tri, scan_prev, sb_tri


def _pack_w_in(w):
    split = 4 * FOX_W
    main = jnp.concatenate([w[:, :split], w[:, split + FOX_HEADS:]], axis=1)
    ff = jnp.pad(w[:, split:split + FOX_HEADS], ((0, 0), (0, FF_PAD - FOX_HEADS)))
    return jnp.concatenate([main, ff], axis=1).astype(jnp.bfloat16)


def _block_diag(w_pool):
    out = jnp.zeros((POOL_W, POOL_W), w_pool.dtype)
    for g in range(POOL_GROUPS):
        lo = g * POOL_GROUP_DIM
        out = out.at[lo:lo + POOL_GROUP_DIM, lo:lo + POOL_GROUP_DIM].set(w_pool[g])
    return out.astype(jnp.bfloat16)


def kernel(x, norm_g, w_in, b_f, q_norm_g, k_norm_g, w_pool, pool_scale, w_out):
    b, s, d = x.shape
    depth = norm_g.shape[0]
    assert d == D_MODEL and s % ROW_BLOCK == 0 and s % ATT_BLOCK == 0
    assert w_in.shape[-1] == C_FF + FOX_HEADS
    hsum, scan_tri, scan_prev, sb_tri = _constants(s)
    nc = s // SCAN_CHUNK
    x2d = x.reshape(b * s, d)
    for l in range(depth):
        w_all = _pack_w_in(w_in[l])
        q_g = jnp.tile(q_norm_g[l], FOX_HEADS)[None, :]
        k_g = jnp.tile(k_norm_g[l], FOX_HEADS)[None, :]
        (fq, fk, fv, fg, px, pg, sq, sk, sv, sg, lf) = _in_proj(
            x2d, norm_g[l][None, :], w_all, hsum, q_g, k_g, b_f[l][None, :])

        lf_rows = lf.reshape(b, s, FOX_HEADS).transpose(0, 2, 1).reshape(b, FOX_HEADS * nc, SCAN_CHUNK)
        c = _forget_scan(lf_rows, scan_tri, scan_prev)
        crow = c.reshape(b, FOX_HEADS // 2, 2, s)
        ccol = crow.transpose(0, 1, 3, 2)

        r3 = lambda a: a.reshape(b, s, a.shape[-1])
        fox_o = _fox_attention(r3(fq), r3(fk), r3(fv), ccol, crow, r3(fg))
        sb_o = _sb_attention(r3(sq), r3(sk), r3(sv), r3(sg), sb_tri)

        x2d = _out_proj(x2d, fox_o.reshape(b * s, FOX_W), sb_o.reshape(b * s, SB_W),
                        px, pg, _block_diag(w_pool[l]), pool_scale[l][None, :],
                        w_out[l].astype(jnp.bfloat16), s)
    return x2d.reshape(b, s, d)
```

```python
import functools

import jax
import jax.numpy as jnp
from jax import lax
from jax.experimental import pallas as pl
from jax.experimental.pallas import tpu as pltpu

D_MODEL = 1024
HEAD_DIM = 64
FOX_HEADS = 8
SB_HEADS = 4
POOL_GROUPS = 4
POOL_WINDOWS = (2, 4, 8, 16)
POOL_GROUP_DIM = 64
FOX_W = FOX_HEADS * HEAD_DIM
SB_W = SB_HEADS * HEAD_DIM
POOL_W = POOL_GROUPS * POOL_GROUP_DIM
D_MIX = FOX_W + POOL_W + SB_W
EPS = 1e-6
NEG = -1e30
SCALE = HEAD_DIM ** -0.5
LOG2E = 1.4426950408889634

LANES = 128
PAIR_W = 2 * HEAD_DIM
MAX_WINDOW = max(POOL_WINDOWS)
FF_PAD = LANES

ROW_BLOCK = 512
ATT_BQ = 1024
ATT_BK = 256
SCAN_CHUNK = LANES

VMEM_LIMIT = 56 * 1024 * 1024

C_FQ, C_FK, C_FV, C_FG = 0, FOX_W, 2 * FOX_W, 3 * FOX_W
C_PX = 4 * FOX_W
C_PG = C_PX + POOL_W
C_SQ = C_PG + POOL_W
C_SK = C_SQ + SB_W
C_SV = C_SK + SB_W
C_SG = C_SV + SB_W
C_FF = C_SG + SB_W
D_IN_PAD = C_FF + FF_PAD


def _bf16_split2(x):
    hi = x.astype(jnp.bfloat16)
    lo = (x - hi.astype(jnp.float32)).astype(jnp.bfloat16)
    return hi, lo


def _bf16_split3(x):
    a = x.astype(jnp.bfloat16)
    r = x - a.astype(jnp.float32)
    b = r.astype(jnp.bfloat16)
    c = (r - b.astype(jnp.float32)).astype(jnp.bfloat16)
    return a, b, c


def _dot(a, b):
    return jnp.dot(a, b, preferred_element_type=jnp.float32)


def _dot_nt(a, b):
    return lax.dot_general(a, b, (((1,), (1,)), ((), ())),
                           preferred_element_type=jnp.float32)


def _silu(x):
    return x * (1.0 / (1.0 + jnp.exp(-x)))


def _softplus(x):
    return jnp.maximum(x, 0.0) + jnp.log(1.0 + jnp.exp2(jnp.abs(x) * (-LOG2E)))


def _in_proj_kernel(x_ref, g_ref, w_ref, hsum_ref, qg_ref, kg_ref, bf_ref,
                    fq_ref, fk_ref, fv_ref, fg_ref, px_ref, pg_ref,
                    sq_ref, sk_ref, sv_ref, sg_ref, lf_ref):
    x = x_ref[...]
    ms = jnp.mean(x * x, axis=-1, keepdims=True)
    h = (x * lax.rsqrt(ms + EPS) * g_ref[...]).astype(jnp.bfloat16)

    def proj(c0, width):
        return _dot(h, w_ref[:, c0:c0 + width])

    def head_norm(y, gain):
        hi, lo = _bf16_split2(y * y)
        ssq = _dot(hi, hsum_ref[...]) + _dot(lo, hsum_ref[...])
        return y * lax.rsqrt(ssq * (1.0 / HEAD_DIM) + EPS) * gain

    fq_ref[...] = (head_norm(proj(C_FQ, FOX_W), qg_ref[...]) * SCALE).astype(jnp.bfloat16)
    fk_ref[...] = head_norm(proj(C_FK, FOX_W), kg_ref[...]).astype(jnp.bfloat16)
    fv_ref[...] = proj(C_FV, FOX_W).astype(jnp.bfloat16)
    fg_ref[...] = proj(C_FG, FOX_W)
    px_ref[...] = proj(C_PX, POOL_W)
    pg_ref[...] = proj(C_PG, POOL_W)
    sq_ref[...] = (proj(C_SQ, SB_W) * SCALE).astype(jnp.bfloat16)
    sk_ref[...] = proj(C_SK, SB_W).astype(jnp.bfloat16)
    sv_ref[...] = proj(C_SV, SB_W).astype(jnp.bfloat16)
    sg_ref[...] = proj(C_SG, SB_W)
    ff = proj(C_FF, FF_PAD)[:, :FOX_HEADS] + bf_ref[...]
    lf_ref[...] = -_softplus(-ff)


def _in_proj(x2d, norm_g, w_all, hsum, q_g, k_g, b_f):
    m = x2d.shape[0]
    bm = ROW_BLOCK
    row = lambda w: pl.BlockSpec((bm, w), lambda i: (i, 0))
    full = lambda a: pl.BlockSpec(a.shape, lambda i: (0,) * a.ndim)
    bf16, f32 = jnp.bfloat16, jnp.float32
    outs = [(FOX_W, bf16), (FOX_W, bf16), (FOX_W, bf16), (FOX_W, f32),
            (POOL_W, f32), (POOL_W, f32),
            (SB_W, bf16), (SB_W, bf16), (SB_W, bf16), (SB_W, f32),
            (FOX_HEADS, f32)]
    return pl.pallas_call(
        _in_proj_kernel,
        grid=(m // bm,),
        in_specs=[row(D_MODEL), full(norm_g), full(w_all), full(hsum),
                  full(q_g), full(k_g), full(b_f)],
        out_specs=[row(w) for w, _ in outs],
        out_shape=[jax.ShapeDtypeStruct((m, w), dt) for w, dt in outs],
        compiler_params=pltpu.CompilerParams(
            dimension_semantics=("arbitrary",), vmem_limit_bytes=VMEM_LIMIT),
    )(x2d, norm_g, w_all, hsum, q_g, k_g, b_f)


def _scan_kernel(lf_ref, tri_ref, prev_ref, c_ref):
    lf = lf_ref[0]
    tri = tri_ref[...]
    within = sum(_dot(p, tri) for p in _bf16_split3(lf))
    total = jnp.broadcast_to(within[:, SCAN_CHUNK - 1:SCAN_CHUNK], within.shape)
    prev = prev_ref[...]
    offset = sum(_dot(prev, p) for p in _bf16_split3(total))
    c_ref[0] = within + offset


def _forget_scan(lf_rows, tri, prev):
    b, r, c = lf_rows.shape
    full = lambda a: pl.BlockSpec(a.shape, lambda i: (0,) * a.ndim)
    blk = pl.BlockSpec((1, r, c), lambda i: (i, 0, 0))
    return pl.pallas_call(
        _scan_kernel,
        grid=(b,),
        in_specs=[blk, full(tri), full(prev)],
        out_specs=blk,
        out_shape=jax.ShapeDtypeStruct((b, r, c), jnp.float32),
        compiler_params=pltpu.CompilerParams(dimension_semantics=("arbitrary",)),
    )(lf_rows, tri, prev)


def _lane_is_first_head(shape):
    return lax.broadcasted_iota(jnp.int32, shape, len(shape) - 1) < HEAD_DIM


def _split_heads(q):
    first = _lane_is_first_head(q.shape)
    zero = jnp.zeros_like(q)
    return jnp.where(first, q, zero), jnp.where(first, zero, q)


def _widen(col, width):
    return jnp.concatenate([col] * (width // LANES), axis=1)


def _fox_kernel(q_ref, k_ref, v_ref, ccol_ref, crow_ref, g_ref, o_ref,
                qs_ref, m_ref, acc_ref, ct_ref):
    bq, bk = ATT_BQ, ATT_BK
    qi = pl.program_id(2)
    q0, q1 = _split_heads(q_ref[0])
    qs_ref[0] = q0
    qs_ref[1] = q1
    for h in range(2):
        m_ref[h] = jnp.full((bq, LANES), NEG, jnp.float32)
        acc_ref[h] = jnp.zeros((bq, PAIR_W), jnp.float32)
        ct_ref[h] = jnp.broadcast_to(ccol_ref[0, 0, :, h:h + 1], (bq, LANES))

    def step(j, r0, masked):
        rows = slice(r0, bq)
        n = bq - r0
        start = pl.multiple_of(j * bk, bk)
        k = k_ref[0, pl.ds(start, bk), :]
        v = v_ref[0, pl.ds(start, bk), :]
        first = _lane_is_first_head(v.shape)
        ones = jnp.ones_like(v)
        vs = (jnp.where(first, v, ones), jnp.where(first, ones, v))
        if masked:
            keep = (lax.broadcasted_iota(jnp.int32, (n, bk), 1)
                    <= lax.broadcasted_iota(jnp.int32, (n, bk), 0))
        for h in range(2):
            cs = crow_ref[0, 0, h:h + 1, pl.ds(start, bk)]
            s = _dot_nt(qs_ref[h, rows, :], k) + (_widen(ct_ref[h, rows, :], bk) - cs)
            if masked:
                s = jnp.where(keep, s, NEG)
            m_prev = m_ref[h, rows, :]
            m_next = jnp.maximum(m_prev, jnp.max(s, axis=1, keepdims=True))
            p = jnp.exp(s - _widen(m_next, bk))
            alpha = jnp.exp(m_prev - m_next)
            acc_ref[h, rows, :] = alpha * acc_ref[h, rows, :] + _dot(p.astype(jnp.bfloat16), vs[h])
            m_ref[h, rows, :] = m_next

    sub = bq // bk

    def body(j, carry):
        step(j, 0, False)
        return carry

    lax.fori_loop(0, qi * sub, body, 0)
    for jj in range(sub):
        step(qi * sub + jj, jj * bk, True)

    a0, a1 = acc_ref[0], acc_ref[1]
    o0 = a0 / pltpu.roll(a0, HEAD_DIM, axis=1)
    o1 = a1 / pltpu.roll(a1, HEAD_DIM, axis=1)
    o = jnp.where(_lane_is_first_head(o0.shape), o0, o1)
    o_ref[0] = (o * _silu(g_ref[0])).astype(o_ref.dtype)


def _fox_attention(q, k, v, ccol, crow, gate):
    b, s, w = q.shape
    t = ATT_BQ
    pairs = w // PAIR_W
    qblk = pl.BlockSpec((1, t, PAIR_W), lambda bi, hp, qi: (bi, qi, hp))
    seq = pl.BlockSpec((1, s, PAIR_W), lambda bi, hp, qi: (bi, 0, hp))
    return pl.pallas_call(
        _fox_kernel,
        grid=(b, pairs, s // t),
        in_specs=[qblk, seq, seq,
                  pl.BlockSpec((1, 1, t, 2), lambda bi, hp, qi: (bi, hp, qi, 0)),
                  pl.BlockSpec((1, 1, 2, s), lambda bi, hp, qi: (bi, hp, 0, 0)),
                  qblk],
        out_specs=qblk,
        out_shape=jax.ShapeDtypeStruct((b, s, w), jnp.bfloat16),
        scratch_shapes=[pltpu.VMEM((2, t, PAIR_W), jnp.bfloat16),
                        pltpu.VMEM((2, t, LANES), jnp.float32),
                        pltpu.VMEM((2, t, PAIR_W), jnp.float32),
                        pltpu.VMEM((2, t, LANES), jnp.float32)],
        compiler_params=pltpu.CompilerParams(
            dimension_semantics=("arbitrary", "arbitrary", "arbitrary"),
            vmem_limit_bytes=VMEM_LIMIT),
    )(q, k, v, ccol, crow, gate)


def _sb_kernel(q_ref, k_ref, v_ref, g_ref, tri_ref, o_ref, qs_ref, r_ref, acc_ref):
    bq, bk = ATT_BQ, ATT_BK
    qi = pl.program_id(2)
    q0, q1 = _split_heads(q_ref[0])
    qs_ref[0] = q0
    qs_ref[1] = q1
    for h in range(2):
        r_ref[h] = jnp.zeros((bq, LANES), jnp.float32)
        acc_ref[h] = jnp.zeros((bq, PAIR_W), jnp.float32)

    def step(j, r0, masked):
        rows = slice(r0, bq)
        n = bq - r0
        start = pl.multiple_of(j * bk, bk)
        k = k_ref[0, pl.ds(start, bk), :]
        v = v_ref[0, pl.ds(start, bk), :]
        tri = tri_ref[...]
        if masked:
            keep = (lax.broadcasted_iota(jnp.int32, (n, bk), 1)
                    < lax.broadcasted_iota(jnp.int32, (n, bk), 0))
        for h in range(2):
            z = _dot_nt(qs_ref[h, rows, :], k)
            sp = _softplus(z)
            if masked:
                sp = jnp.where(keep, sp, 0.0)
            hi, lo = _bf16_split2(sp)
            cum = _dot(hi, tri) + _dot(lo, tri)
            r_prev = r_ref[h, rows, :]
            a = jnp.exp(z - cum - _widen(r_prev, bk))
            if masked:
                a = jnp.where(keep, a, 0.0)
            acc_ref[h, rows, :] = acc_ref[h, rows, :] + _dot(a.astype(jnp.bfloat16), v)
            r_ref[h, rows, :] = r_prev + jnp.broadcast_to(cum[:, 0:1], (n, LANES))

    sub = bq // bk
    for jj in range(sub - 1, -1, -1):
        step(qi * sub + jj, jj * bk, True)

    def body(i, carry):
        step(qi * sub - 1 - i, 0, False)
        return carry

    lax.fori_loop(0, qi * sub, body, 0)

    o = jnp.where(_lane_is_first_head((bq, PAIR_W)), acc_ref[0], acc_ref[1])
    o_ref[0] = (o * _silu(g_ref[0])).astype(o_ref.dtype)


def _sb_attention(q, k, v, gate, tri):
    b, s, w = q.shape
    t = ATT_BQ
    pairs = w // PAIR_W
    qblk = pl.BlockSpec((1, t, PAIR_W), lambda bi, hp, qi: (bi, qi, hp))
    seq = pl.BlockSpec((1, s, PAIR_W), lambda bi, hp, qi: (bi, 0, hp))
    return pl.pallas_call(
        _sb_kernel,
        grid=(b, pairs, s // t),
        in_specs=[qblk, seq, seq, qblk,
                  pl.BlockSpec(tri.shape, lambda bi, hp, qi: (0, 0))],
        out_specs=qblk,
        out_shape=jax.ShapeDtypeStruct((b, s, w), jnp.bfloat16),
        scratch_shapes=[pltpu.VMEM((2, t, PAIR_W), jnp.bfloat16),
                        pltpu.VMEM((2, t, LANES), jnp.float32),
                        pltpu.VMEM((2, t, PAIR_W), jnp.float32)],
        compiler_params=pltpu.CompilerParams(
            dimension_semantics=("arbitrary", "arbitrary", "arbitrary"),
            vmem_limit_bytes=VMEM_LIMIT),
    )(q, k, v, gate, tri)


def _out_proj_kernel(x_ref, fox_ref, sb_ref, px_ref, hist_ref, pg_ref,
                     wpool_ref, pscale_ref, wo_ref, o_ref, xp_ref, *, blocks_per_seq):
    bm = ROW_BLOCK
    i = pl.program_id(0)
    pos0 = (i % blocks_per_seq) * bm
    hist = hist_ref[...]
    xp_ref[0:MAX_WINDOW, :] = jnp.where(pos0 == 0, jnp.zeros_like(hist), hist)
    px = px_ref[...]
    xp_ref[MAX_WINDOW:MAX_WINDOW + bm, :] = px

    sums = {}
    run = px
    for d in range(1, MAX_WINDOW):
        run = run + xp_ref[MAX_WINDOW - d:MAX_WINDOW - d + bm, :]
        if d + 1 in POOL_WINDOWS:
            sums[d + 1] = run
    group = lax.broadcasted_iota(jnp.int32, (bm, POOL_W), 1) // POOL_GROUP_DIM
    pos = pos0 + lax.broadcasted_iota(jnp.int32, (bm, POOL_W), 0)
    wsum = sums[POOL_WINDOWS[-1]]
    win = jnp.full((bm, POOL_W), POOL_WINDOWS[-1], jnp.int32)
    for g in range(POOL_GROUPS - 2, -1, -1):
        wsum = jnp.where(group == g, sums[POOL_WINDOWS[g]], wsum)
        win = jnp.where(group == g, POOL_WINDOWS[g], win)
    count = jnp.minimum(pos + 1, win).astype(jnp.float32)
    pooled = wsum / count - px
    y = _dot(pooled.astype(jnp.bfloat16), wpool_ref[...]) * pscale_ref[...]
    pool_out = (y * _silu(pg_ref[...])).astype(jnp.bfloat16)

    o_ref[...] = (x_ref[...]
                  + _dot(fox_ref[...], wo_ref[0:FOX_W, :])
                  + _dot(pool_out, wo_ref[FOX_W:FOX_W + POOL_W, :])
                  + _dot(sb_ref[...], wo_ref[FOX_W + POOL_W:D_MIX, :]))


def _out_proj(x2d, fox_o, sb_o, px, pg, wpool_bd, pscale, w_out, seq_len):
    m = x2d.shape[0]
    bm = ROW_BLOCK
    row = lambda w: pl.BlockSpec((bm, w), lambda i: (i, 0))
    full = lambda a: pl.BlockSpec(a.shape, lambda i: (0,) * a.ndim)
    hist_per_block = bm // MAX_WINDOW
    hist = pl.BlockSpec((MAX_WINDOW, POOL_W),
                        lambda i: (jnp.maximum(i * hist_per_block - 1, 0), 0))
    kern = functools.partial(_out_proj_kernel, blocks_per_seq=seq_len // bm)
    return pl.pallas_call(
        kern,
        grid=(m // bm,),
        in_specs=[row(D_MODEL), row(FOX_W), row(SB_W), row(POOL_W), hist, row(POOL_W),
                  full(wpool_bd), full(pscale), full(w_out)],
        out_specs=row(D_MODEL),
        out_shape=jax.ShapeDtypeStruct((m, D_MODEL), jnp.float32),
        scratch_shapes=[pltpu.VMEM((MAX_WINDOW + bm, POOL_W), jnp.float32)],
        compiler_params=pltpu.CompilerParams(
            dimension_semantics=("arbitrary",), vmem_limit_bytes=VMEM_LIMIT),
    )(x2d, fox_o, sb_o, px, px, pg, wpool_bd, pscale, w_out)


def _constants(seq_len):
    bf16 = jnp.bfloat16
    idx = jnp.arange(FOX_W)
    hsum = (idx[:, None] // HEAD_DIM == idx[None, :] // HEAD_DIM).astype(bf16)
    c = jnp.arange(SCAN_CHUNK)
    scan_tri = (c[:, None] <= c[None, :]).astype(bf16)
    nc = seq_len // SCAN_CHUNK
    r = jnp.arange(FOX_HEADS * nc)
    scan_prev = ((r[:, None] // nc == r[None, :] // nc)
                 & (r[None, :] < r[:, None])).astype(bf16)
    a = jnp.arange(ATT_BK)
    sb_tri = (a[:, None] >= a[None, :]).astype(bf16)
    return hsum, scan_tri, scan_prev, sb_tri


def _pack_w_in(w):
    split = 4 * FOX_W
    main = jnp.concatenate([w[:, :split], w[:, split + FOX_HEADS:]], axis=1)
    ff = jnp.pad(w[:, split:split + FOX_HEADS], ((0, 0), (0, FF_PAD - FOX_HEADS)))
    return jnp.concatenate([main, ff], axis=1).astype(jnp.bfloat16)


def _block_diag(w_pool):
    out = jnp.zeros((POOL_W, POOL_W), w_pool.dtype)
    for g in range(POOL_GROUPS):
        lo = g * POOL_GROUP_DIM
        out = out.at[lo:lo + POOL_GROUP_DIM, lo:lo + POOL_GROUP_DIM].set(w_pool[g])
    return out.astype(jnp.bfloat16)


def kernel(x, norm_g, w_in, b_f, q_norm_g, k_norm_g, w_pool, pool_scale, w_out):
    b, s, d = x.shape
    depth = norm_g.shape[0]
    assert d == D_MODEL and s % ROW_BLOCK == 0 and s % ATT_BQ == 0 and ATT_BQ % ATT_BK == 0
    assert w_in.shape[-1] == C_FF + FOX_HEADS
    hsum, scan_tri, scan_prev, sb_tri = _constants(s)
    nc = s // SCAN_CHUNK
    x2d = x.reshape(b * s, d)
    for l in range(depth):
        w_all = _pack_w_in(w_in[l])
        q_g = jnp.tile(q_norm_g[l], FOX_HEADS)[None, :]
        k_g = jnp.tile(k_norm_g[l], FOX_HEADS)[None, :]
        (fq, fk, fv, fg, px, pg, sq, sk, sv, sg, lf) = _in_proj(
            x2d, norm_g[l][None, :], w_all, hsum, q_g, k_g, b_f[l][None, :])

        lf_rows = lf.reshape(b, s, FOX_HEADS).transpose(0, 2, 1).reshape(b, FOX_HEADS * nc, SCAN_CHUNK)
        c = _forget_scan(lf_rows, scan_tri, scan_prev)
        crow = c.reshape(b, FOX_HEADS // 2, 2, s)
        ccol = crow.transpose(0, 1, 3, 2)

        r3 = lambda a: a.reshape(b, s, a.shape[-1])
        fox_o = _fox_attention(r3(fq), r3(fk), r3(fv), ccol, crow, r3(fg))
        sb_o = _sb_attention(r3(sq), r3(sk), r3(sv), r3(sg), sb_tri)

        x2d = _out_proj(x2d, fox_o.reshape(b * s, FOX_W), sb_o.reshape(b * s, SB_W),
                        px, pg, _block_diag(w_pool[l]), pool_scale[l][None, :],
                        w_out[l].astype(jnp.bfloat16), s)
    return x2d.reshape(b, s, d)
```

```python
import functools

import jax
import jax.numpy as jnp
from jax import lax
from jax.experimental import pallas as pl
from jax.experimental.pallas import tpu as pltpu

D_MODEL = 1024
HEAD_DIM = 64
FOX_HEADS = 8
SB_HEADS = 4
POOL_GROUPS = 4
POOL_WINDOWS = (2, 4, 8, 16)
POOL_GROUP_DIM = 64
FOX_W = FOX_HEADS * HEAD_DIM
SB_W = SB_HEADS * HEAD_DIM
POOL_W = POOL_GROUPS * POOL_GROUP_DIM
D_MIX = FOX_W + POOL_W + SB_W
EPS = 1e-6
NEG = -1e30
SCALE = HEAD_DIM ** -0.5
LOG2E = 1.4426950408889634
FOX_MAX_BOUND = 30.0
SB_CUTOFF = 106.0

LANES = 128
PAIR_W = 2 * HEAD_DIM
MAX_WINDOW = max(POOL_WINDOWS)
FF_PAD = LANES

ROW_BLOCK = 512
ATT_BQ = 1024
ATT_BK = 256
FOX_BK_FULL = 512
SB_UNROLL = 2
SCAN_CHUNK = LANES

VMEM_LIMIT = 56 * 1024 * 1024

C_FQ, C_FK, C_FV, C_FG = 0, FOX_W, 2 * FOX_W, 3 * FOX_W
C_PX = 4 * FOX_W
C_PG = C_PX + POOL_W
C_SQ = C_PG + POOL_W
C_SK = C_SQ + SB_W
C_SV = C_SK + SB_W
C_SG = C_SV + SB_W
C_FF = C_SG + SB_W
D_IN_PAD = C_FF + FF_PAD


def _bf16_split2(x):
    hi = x.astype(jnp.bfloat16)
    lo = (x - hi.astype(jnp.float32)).astype(jnp.bfloat16)
    return hi, lo


def _bf16_split3(x):
    a = x.astype(jnp.bfloat16)
    r = x - a.astype(jnp.float32)
    b = r.astype(jnp.bfloat16)
    c = (r - b.astype(jnp.float32)).astype(jnp.bfloat16)
    return a, b, c


def _dot(a, b):
    return jnp.dot(a, b, preferred_element_type=jnp.float32)


def _dot_nt(a, b):
    return lax.dot_general(a, b, (((1,), (1,)), ((), ())),
                           preferred_element_type=jnp.float32)


def _silu(x):
    return x * (1.0 / (1.0 + jnp.exp(-x)))


def _softplus(x):
    return jnp.maximum(x, 0.0) + jnp.log(1.0 + jnp.exp2(jnp.abs(x) * (-LOG2E)))


def _in_proj_kernel(x_ref, g_ref, w_ref, hsum_ref, qg_ref, kg_ref, bf_ref,
                    fq_ref, fk_ref, fv_ref, fg_ref, px_ref, pg_ref,
                    sq_ref, sk_ref, sv_ref, sg_ref, lf_ref):
    x = x_ref[...]
    ms = jnp.mean(x * x, axis=-1, keepdims=True)
    h = (x * lax.rsqrt(ms + EPS) * g_ref[...]).astype(jnp.bfloat16)

    def proj(c0, width):
        return _dot(h, w_ref[:, c0:c0 + width])

    def head_norm(y, gain):
        hi, lo = _bf16_split2(y * y)
        ssq = _dot(hi, hsum_ref[...]) + _dot(lo, hsum_ref[...])
        return y * lax.rsqrt(ssq * (1.0 / HEAD_DIM) + EPS) * gain

    fq_ref[...] = (head_norm(proj(C_FQ, FOX_W), qg_ref[...]) * SCALE).astype(jnp.bfloat16)
    fk_ref[...] = head_norm(proj(C_FK, FOX_W), kg_ref[...]).astype(jnp.bfloat16)
    fv_ref[...] = proj(C_FV, FOX_W).astype(jnp.bfloat16)
    fg_ref[...] = proj(C_FG, FOX_W)
    px_ref[...] = proj(C_PX, POOL_W)
    pg_ref[...] = proj(C_PG, POOL_W)
    sq_ref[...] = (proj(C_SQ, SB_W) * SCALE).astype(jnp.bfloat16)
    sk_ref[...] = proj(C_SK, SB_W).astype(jnp.bfloat16)
    sv_ref[...] = proj(C_SV, SB_W).astype(jnp.bfloat16)
    sg_ref[...] = proj(C_SG, SB_W)
    ff = proj(C_FF, FF_PAD)[:, :FOX_HEADS] + bf_ref[...]
    lf_ref[...] = -_softplus(-ff)


def _in_proj(x2d, norm_g, w_all, hsum, q_g, k_g, b_f):
    m = x2d.shape[0]
    bm = ROW_BLOCK
    row = lambda w: pl.BlockSpec((bm, w), lambda i: (i, 0))
    full = lambda a: pl.BlockSpec(a.shape, lambda i: (0,) * a.ndim)
    bf16, f32 = jnp.bfloat16, jnp.float32
    outs = [(FOX_W, bf16), (FOX_W, bf16), (FOX_W, bf16), (FOX_W, f32),
            (POOL_W, f32), (POOL_W, f32),
            (SB_W, bf16), (SB_W, bf16), (SB_W, bf16), (SB_W, f32),
            (FOX_HEADS, f32)]
    return pl.pallas_call(
        _in_proj_kernel,
        grid=(m // bm,),
        in_specs=[row(D_MODEL), full(norm_g), full(w_all), full(hsum),
                  full(q_g), full(k_g), full(b_f)],
        out_specs=[row(w) for w, _ in outs],
        out_shape=[jax.ShapeDtypeStruct((m, w), dt) for w, dt in outs],
        compiler_params=pltpu.CompilerParams(
            dimension_semantics=("arbitrary",), vmem_limit_bytes=VMEM_LIMIT),
    )(x2d, norm_g, w_all, hsum, q_g, k_g, b_f)


def _scan_kernel(lf_ref, tri_ref, prev_ref, c_ref):
    lf = lf_ref[0]
    tri = tri_ref[...]
    within = sum(_dot(p, tri) for p in _bf16_split3(lf))
    total = jnp.broadcast_to(within[:, SCAN_CHUNK - 1:SCAN_CHUNK], within.shape)
    prev = prev_ref[...]
    offset = sum(_dot(prev, p) for p in _bf16_split3(total))
    c_ref[0] = within + offset


def _forget_scan(lf_rows, tri, prev):
    b, r, c = lf_rows.shape
    full = lambda a: pl.BlockSpec(a.shape, lambda i: (0,) * a.ndim)
    blk = pl.BlockSpec((1, r, c), lambda i: (i, 0, 0))
    return pl.pallas_call(
        _scan_kernel,
        grid=(b,),
        in_specs=[blk, full(tri), full(prev)],
        out_specs=blk,
        out_shape=jax.ShapeDtypeStruct((b, r, c), jnp.float32),
        compiler_params=pltpu.CompilerParams(dimension_semantics=("arbitrary",)),
    )(lf_rows, tri, prev)


def _lane_is_first_head(shape):
    return lax.broadcasted_iota(jnp.int32, shape, len(shape) - 1) < HEAD_DIM


def _split_heads(q):
    first = _lane_is_first_head(q.shape)
    zero = jnp.zeros_like(q)
    return jnp.where(first, q, zero), jnp.where(first, zero, q)


def _widen(col, width):
    return jnp.concatenate([col] * (width // LANES), axis=1)


def _fox_kernel(bound_ref, q_ref, k_ref, v_ref, ccol_ref, crow_ref, g_ref, o_ref,
                qs_ref, m_ref, acc_ref, ct_ref, *, running_max):
    bq, bk = ATT_BQ, ATT_BK
    qi = pl.program_id(2)
    q0, q1 = _split_heads(q_ref[0])
    qs_ref[0] = q0
    qs_ref[1] = q1
    shift = 0.0 if running_max else bound_ref[0, 0]
    for h in range(2):
        m_ref[h] = jnp.full((bq, LANES), NEG, jnp.float32)
        acc_ref[h] = jnp.zeros((bq, PAIR_W), jnp.float32)
        ct_ref[h] = jnp.broadcast_to(ccol_ref[0, 0, :, h:h + 1], (bq, LANES)) - shift

    def step(j, bk, r0, masked):
        rows = slice(r0, bq)
        n = bq - r0
        start = pl.multiple_of(j * bk, bk)
        k = k_ref[0, pl.ds(start, bk), :]
        v = v_ref[0, pl.ds(start, bk), :]
        first = _lane_is_first_head(v.shape)
        ones = jnp.ones_like(v)
        vs = (jnp.where(first, v, ones), jnp.where(first, ones, v))
        if masked:
            keep = (lax.broadcasted_iota(jnp.int32, (n, bk), 1)
                    <= lax.broadcasted_iota(jnp.int32, (n, bk), 0))
        for h in range(2):
            cs = crow_ref[0, 0, h:h + 1, pl.ds(start, bk)]
            s = _dot_nt(qs_ref[h, rows, :], k) + (_widen(ct_ref[h, rows, :], bk) - cs)
            if masked:
                s = jnp.where(keep, s, NEG)
            if running_max:
                m_prev = m_ref[h, rows, :]
                m_next = jnp.maximum(m_prev, jnp.max(s, axis=1, keepdims=True))
                p = jnp.exp(s - _widen(m_next, bk))
                acc = jnp.exp(m_prev - m_next) * acc_ref[h, rows, :]
                m_ref[h, rows, :] = m_next
            else:
                p = jnp.exp(s)
                acc = acc_ref[h, rows, :]
            acc_ref[h, rows, :] = acc + _dot(p.astype(jnp.bfloat16), vs[h])

    def body(j, carry):
        step(j, FOX_BK_FULL, 0, False)
        return carry

    lax.fori_loop(0, qi * (bq // FOX_BK_FULL), body, 0)
    sub = bq // bk
    for jj in range(sub):
        step(qi * sub + jj, bk, jj * bk, True)

    a0, a1 = acc_ref[0], acc_ref[1]
    o0 = a0 / pltpu.roll(a0, HEAD_DIM, axis=1)
    o1 = a1 / pltpu.roll(a1, HEAD_DIM, axis=1)
    o = jnp.where(_lane_is_first_head(o0.shape), o0, o1)
    o_ref[0] = (o * _silu(g_ref[0])).astype(o_ref.dtype)


def _fox_attention(bound, q, k, v, ccol, crow, gate, *, running_max):
    b, s, w = q.shape
    t = ATT_BQ
    pairs = w // PAIR_W
    qblk = pl.BlockSpec((1, t, PAIR_W), lambda bi, hp, qi: (bi, qi, hp))
    seq = pl.BlockSpec((1, s, PAIR_W), lambda bi, hp, qi: (bi, 0, hp))
    return pl.pallas_call(
        functools.partial(_fox_kernel, running_max=running_max),
        grid=(b, pairs, s // t),
        in_specs=[pl.BlockSpec(memory_space=pltpu.SMEM), qblk, seq, seq,
                  pl.BlockSpec((1, 1, t, 2), lambda bi, hp, qi: (bi, hp, qi, 0)),
                  pl.BlockSpec((1, 1, 2, s), lambda bi, hp, qi: (bi, hp, 0, 0)),
                  qblk],
        out_specs=qblk,
        out_shape=jax.ShapeDtypeStruct((b, s, w), jnp.bfloat16),
        scratch_shapes=[pltpu.VMEM((2, t, PAIR_W), jnp.bfloat16),
                        pltpu.VMEM((2, t, LANES), jnp.float32),
                        pltpu.VMEM((2, t, PAIR_W), jnp.float32),
                        pltpu.VMEM((2, t, LANES), jnp.float32)],
        compiler_params=pltpu.CompilerParams(
            dimension_semantics=("arbitrary", "arbitrary", "arbitrary"),
            vmem_limit_bytes=VMEM_LIMIT),
    )(bound, q, k, v, ccol, crow, gate)


def _fox_logit_bound(q_gain, k_gain):
    gq = jnp.max(jnp.abs(q_gain))
    gk = jnp.max(jnp.abs(k_gain))
    return (HEAD_DIM * SCALE * 1.01) * gq * gk + 0.05


def _sb_kernel(q_ref, k_ref, v_ref, g_ref, tri_ref, o_ref, qs_ref, r_ref, acc_ref):
    bq, bk = ATT_BQ, ATT_BK
    qi = pl.program_id(2)
    q0, q1 = _split_heads(q_ref[0])
    qs_ref[0] = q0
    qs_ref[1] = q1
    for h in range(2):
        r_ref[h] = jnp.zeros((bq, LANES), jnp.float32)
        acc_ref[h] = jnp.zeros((bq, PAIR_W), jnp.float32)

    def step(j, r0, masked):
        rows = slice(r0, bq)
        n = bq - r0
        start = pl.multiple_of(j * bk, bk)
        k = k_ref[0, pl.ds(start, bk), :]
        v = v_ref[0, pl.ds(start, bk), :]
        tri = tri_ref[...]
        if masked:
            keep = (lax.broadcasted_iota(jnp.int32, (n, bk), 1)
                    < lax.broadcasted_iota(jnp.int32, (n, bk), 0))
        for h in range(2):
            z = _dot_nt(qs_ref[h, rows, :], k)
            sp = _softplus(z)
            if masked:
                sp = jnp.where(keep, sp, 0.0)
            hi, lo = _bf16_split2(sp)
            cum = _dot(hi, tri) + _dot(lo, tri)
            a = jnp.exp(z - cum)
            if masked:
                a = jnp.where(keep, a, 0.0)
            r_prev = r_ref[h, rows, :]
            acc_ref[h, rows, :] = (acc_ref[h, rows, :]
                                   + jnp.exp(-r_prev) * _dot(a.astype(jnp.bfloat16), v))
            r_ref[h, rows, :] = r_prev + jnp.broadcast_to(cum[:, 0:1], (n, LANES))

    sub = bq // bk
    for jj in range(sub - 1, -1, -1):
        step(qi * sub + jj, jj * bk, True)

    n_full = qi * sub
    assert sub % SB_UNROLL == 0

    def cond(carry):
        i, done = carry
        return jnp.logical_and(i < n_full, jnp.logical_not(done))

    def body(carry):
        i, _ = carry
        for u in range(SB_UNROLL):
            step(n_full - 1 - i - u, 0, False)
        return i + SB_UNROLL, jnp.min(r_ref[...]) >= SB_CUTOFF

    lax.while_loop(cond, body, (jnp.int32(0), jnp.bool_(False)))

    o = jnp.where(_lane_is_first_head((bq, PAIR_W)), acc_ref[0], acc_ref[1])
    o_ref[0] = (o * _silu(g_ref[0])).astype(o_ref.dtype)


def _sb_attention(q, k, v, gate, tri):
    b, s, w = q.shape
    t = ATT_BQ
    pairs = w // PAIR_W
    qblk = pl.BlockSpec((1, t, PAIR_W), lambda bi, hp, qi: (bi, qi, hp))
    seq = pl.BlockSpec((1, s, PAIR_W), lambda bi, hp, qi: (bi, 0, hp))
    return pl.pallas_call(
        _sb_kernel,
        grid=(b, pairs, s // t),
        in_specs=[qblk, seq, seq, qblk,
                  pl.BlockSpec(tri.shape, lambda bi, hp, qi: (0, 0))],
        out_specs=qblk,
        out_shape=jax.ShapeDtypeStruct((b, s, w), jnp.bfloat16),
        scratch_shapes=[pltpu.VMEM((2, t, PAIR_W), jnp.bfloat16),
                        pltpu.VMEM((2, t, LANES), jnp.float32),
                        pltpu.VMEM((2, t, PAIR_W), jnp.float32)],
        compiler_params=pltpu.CompilerParams(
            dimension_semantics=("arbitrary", "arbitrary", "arbitrary"),
            vmem_limit_bytes=VMEM_LIMIT),
    )(q, k, v, gate, tri)


def _out_proj_kernel(x_ref, fox_ref, sb_ref, px_ref, hist_ref, pg_ref,
                     wpool_ref, pscale_ref, wo_ref, o_ref, xp_ref, *, blocks_per_seq):
    bm = ROW_BLOCK
    i = pl.program_id(0)
    pos0 = (i % blocks_per_seq) * bm
    hist = hist_ref[...]
    xp_ref[0:MAX_WINDOW, :] = jnp.where(pos0 == 0, jnp.zeros_like(hist), hist)
    px = px_ref[...]
    xp_ref[MAX_WINDOW:MAX_WINDOW + bm, :] = px

    sums = {}
    run = px
    for d in range(1, MAX_WINDOW):
        run = run + xp_ref[MAX_WINDOW - d:MAX_WINDOW - d + bm, :]
        if d + 1 in POOL_WINDOWS:
            sums[d + 1] = run
    group = lax.broadcasted_iota(jnp.int32, (bm, POOL_W), 1) // POOL_GROUP_DIM
    pos = pos0 + lax.broadcasted_iota(jnp.int32, (bm, POOL_W), 0)
    wsum = sums[POOL_WINDOWS[-1]]
    win = jnp.full((bm, POOL_W), POOL_WINDOWS[-1], jnp.int32)
    for g in range(POOL_GROUPS - 2, -1, -1):
        wsum = jnp.where(group == g, sums[POOL_WINDOWS[g]], wsum)
        win = jnp.where(group == g, POOL_WINDOWS[g], win)
    count = jnp.minimum(pos + 1, win).astype(jnp.float32)
    pooled = wsum / count - px
    y = _dot(pooled.astype(jnp.bfloat16), wpool_ref[...]) * pscale_ref[...]
    pool_out = (y * _silu(pg_ref[...])).astype(jnp.bfloat16)

    o_ref[...] = (x_ref[...]
                  + _dot(fox_ref[...], wo_ref[0:FOX_W, :])
                  + _dot(pool_out, wo_ref[FOX_W:FOX_W + POOL_W, :])
                  + _dot(sb_ref[...], wo_ref[FOX_W + POOL_W:D_MIX, :]))


def _out_proj(x2d, fox_o, sb_o, px, pg, wpool_bd, pscale, w_out, seq_len):
    m = x2d.shape[0]
    bm = ROW_BLOCK
    row = lambda w: pl.BlockSpec((bm, w), lambda i: (i, 0))
    full = lambda a: pl.BlockSpec(a.shape, lambda i: (0,) * a.ndim)
    hist_per_block = bm // MAX_WINDOW
    hist = pl.BlockSpec((MAX_WINDOW, POOL_W),
                        lambda i: (jnp.maximum(i * hist_per_block - 1, 0), 0))
    kern = functools.partial(_out_proj_kernel, blocks_per_seq=seq_len // bm)
    return pl.pallas_call(
        kern,
        grid=(m // bm,),
        in_specs=[row(D_MODEL), row(FOX_W), row(SB_W), row(POOL_W), hist, row(POOL_W),
                  full(wpool_bd), full(pscale), full(w_out)],
        out_specs=row(D_MODEL),
        out_shape=jax.ShapeDtypeStruct((m, D_MODEL), jnp.float32),
        scratch_shapes=[pltpu.VMEM((MAX_WINDOW + bm, POOL_W), jnp.float32)],
        compiler_params=pltpu.CompilerParams(
            dimension_semantics=("arbitrary",), vmem_limit_bytes=VMEM_LIMIT),
    )(x2d, fox_o, sb_o, px, px, pg, wpool_bd, pscale, w_out)


def _constants(seq_len):
    bf16 = jnp.bfloat16
    idx = jnp.arange(FOX_W)
    hsum = (idx[:, None] // HEAD_DIM == idx[None, :] // HEAD_DIM).astype(bf16)
    c = jnp.arange(SCAN_CHUNK)
    scan_tri = (c[:, None] <= c[None, :]).astype(bf16)
    nc = seq_len // SCAN_CHUNK
    r = jnp.arange(FOX_HEADS * nc)
    scan_prev = ((r[:, None] // nc == r[None, :] // nc)
                 & (r[None, :] < r[:, None])).astype(bf16)
    a = jnp.arange(ATT_BK)
    sb_tri = (a[:, None] >= a[None, :]).astype(bf16)
    return hsum, scan_tri, scan_prev, sb_tri


def _pack_w_in(w):
    split = 4 * FOX_W
    main = jnp.concatenate([w[:, :split], w[:, split + FOX_HEADS:]], axis=1)
    ff = jnp.pad(w[:, split:split + FOX_HEADS], ((0, 0), (0, FF_PAD - FOX_HEADS)))
    return jnp.concatenate([main, ff], axis=1).astype(jnp.bfloat16)


def _block_diag(w_pool):
    out = jnp.zeros((POOL_W, POOL_W), w_pool.dtype)
    for g in range(POOL_GROUPS):
        lo = g * POOL_GROUP_DIM
        out = out.at[lo:lo + POOL_GROUP_DIM, lo:lo + POOL_GROUP_DIM].set(w_pool[g])
    return out.astype(jnp.bfloat16)


def kernel(x, norm_g, w_in, b_f, q_norm_g, k_norm_g, w_pool, pool_scale, w_out):
    b, s, d = x.shape
    depth = norm_g.shape[0]
    assert d == D_MODEL and s % ROW_BLOCK == 0 and s % ATT_BQ == 0 and ATT_BQ % ATT_BK == 0
    assert w_in.shape[-1] == C_FF + FOX_HEADS
    hsum, scan_tri, scan_prev, sb_tri = _constants(s)
    nc = s // SCAN_CHUNK
    x2d = x.reshape(b * s, d)
    for l in range(depth):
        w_all = _pack_w_in(w_in[l])
        q_g = jnp.tile(q_norm_g[l], FOX_HEADS)[None, :]
        k_g = jnp.tile(k_norm_g[l], FOX_HEADS)[None, :]
        (fq, fk, fv, fg, px, pg, sq, sk, sv, sg, lf) = _in_proj(
            x2d, norm_g[l][None, :], w_all, hsum, q_g, k_g, b_f[l][None, :])

        lf_rows = lf.reshape(b, s, FOX_HEADS).transpose(0, 2, 1).reshape(b, FOX_HEADS * nc, SCAN_CHUNK)
        c = _forget_scan(lf_rows, scan_tri, scan_prev)
        crow = c.reshape(b, FOX_HEADS // 2, 2, s)
        ccol = crow.transpose(0, 1, 3, 2)

        r3 = lambda a: a.reshape(b, s, a.shape[-1])
        bound = _fox_logit_bound(q_norm_g[l], k_norm_g[l])
        fox_o = lax.cond(
            bound <= FOX_MAX_BOUND,
            functools.partial(_fox_attention, running_max=False),
            functools.partial(_fox_attention, running_max=True),
            bound.reshape(1, 1), r3(fq), r3(fk), r3(fv), ccol, crow, r3(fg))
        sb_o = _sb_attention(r3(sq), r3(sk), r3(sv), r3(sg), sb_tri)

        x2d = _out_proj(x2d, fox_o.reshape(b * s, FOX_W), sb_o.reshape(b * s, SB_W),
                        px, pg, _block_diag(w_pool[l]), pool_scale[l][None, :],
                        w_out[l].astype(jnp.bfloat16), s)
    return x2d.reshape(b, s, d)
```

```python
import functools

import jax
import jax.numpy as jnp
from jax import lax
from jax.experimental import pallas as pl
from jax.experimental.pallas import tpu as pltpu

D_MODEL = 1024
HEAD_DIM = 64
FOX_HEADS = 8
SB_HEADS = 4
POOL_GROUPS = 4
POOL_WINDOWS = (2, 4, 8, 16)
POOL_GROUP_DIM = 64
FOX_W = FOX_HEADS * HEAD_DIM
SB_W = SB_HEADS * HEAD_DIM
POOL_W = POOL_GROUPS * POOL_GROUP_DIM
D_MIX = FOX_W + POOL_W + SB_W
EPS = 1e-6
NEG = -1e30
SCALE = HEAD_DIM ** -0.5
LOG2E = 1.4426950408889634
FOX_MAX_BOUND = 30.0
SB_CUTOFF = 106.0

LANES = 128
MXU_TILE = 256
PAIR_W = 2 * HEAD_DIM
MAX_WINDOW = max(POOL_WINDOWS)
FF_PAD = LANES

ROW_BLOCK = 512
ATT_BQ = 1024
ATT_BK = 256
FOX_BK_FULL = 512
FOX_BLOCKS_PER_TRIP = 2
SCAN_CHUNK = LANES

VMEM_LIMIT = 56 * 1024 * 1024

C_FQ, C_FK, C_FV, C_FG = 0, FOX_W, 2 * FOX_W, 3 * FOX_W
C_PX = 4 * FOX_W
C_PG = C_PX + POOL_W
C_SQ = C_PG + POOL_W
C_SK = C_SQ + SB_W
C_SV = C_SK + SB_W
C_SG = C_SV + SB_W
C_FF = C_SG + SB_W
D_IN_PAD = C_FF + FF_PAD


def _bf16_split2(x):
    hi = x.astype(jnp.bfloat16)
    lo = (x - hi.astype(jnp.float32)).astype(jnp.bfloat16)
    return hi, lo


def _bf16_split3(x):
    a = x.astype(jnp.bfloat16)
    r = x - a.astype(jnp.float32)
    b = r.astype(jnp.bfloat16)
    c = (r - b.astype(jnp.float32)).astype(jnp.bfloat16)
    return a, b, c


def _dot(a, b):
    return jnp.dot(a, b, preferred_element_type=jnp.float32)


def _dot_nt(a, b):
    return lax.dot_general(a, b, (((1,), (1,)), ((), ())),
                           preferred_element_type=jnp.float32)


def _silu(x):
    return x * (1.0 / (1.0 + jnp.exp(-x)))


def _softplus(x):
    return jnp.maximum(x, 0.0) + jnp.log(1.0 + jnp.exp2(jnp.abs(x) * (-LOG2E)))


def _in_proj_kernel(x_ref, g_ref, w_ref, hsum_ref, qg_ref, kg_ref, bf_ref,
                    fq_ref, fk_ref, fv_ref, fg_ref, px_ref, pg_ref,
                    sq_ref, sk_ref, sv_ref, sg_ref, lf_ref):
    x = x_ref[...]
    ms = jnp.mean(x * x, axis=-1, keepdims=True)
    h = (x * lax.rsqrt(ms + EPS) * g_ref[...]).astype(jnp.bfloat16)

    def proj(c0, width):
        return _dot(h, w_ref[:, c0:c0 + width])

    def head_norm(y, gain):
        hi, lo = _bf16_split2(y * y)
        hsum = hsum_ref[...]
        w = hsum.shape[0]
        ssq = jnp.concatenate(
            [_dot(hi[:, c:c + w], hsum) + _dot(lo[:, c:c + w], hsum)
             for c in range(0, y.shape[1], w)], axis=1)
        return y * lax.rsqrt(ssq * (1.0 / HEAD_DIM) + EPS) * gain

    fq_ref[...] = (head_norm(proj(C_FQ, FOX_W), qg_ref[...]) * SCALE).astype(jnp.bfloat16)
    fk_ref[...] = head_norm(proj(C_FK, FOX_W), kg_ref[...]).astype(jnp.bfloat16)
    fv_ref[...] = proj(C_FV, FOX_W).astype(jnp.bfloat16)
    fg_ref[...] = proj(C_FG, FOX_W)
    px_ref[...] = proj(C_PX, POOL_W)
    pg_ref[...] = proj(C_PG, POOL_W)
    sq_ref[...] = (proj(C_SQ, SB_W) * SCALE).astype(jnp.bfloat16)
    sk_ref[...] = proj(C_SK, SB_W).astype(jnp.bfloat16)
    sv_ref[...] = proj(C_SV, SB_W).astype(jnp.bfloat16)
    sg_ref[...] = proj(C_SG, SB_W)
    ff = proj(C_FF, FF_PAD)[:, :FOX_HEADS] + bf_ref[...]
    lf_ref[...] = -_softplus(-ff)


def _in_proj(x2d, norm_g, w_all, hsum, q_g, k_g, b_f):
    m = x2d.shape[0]
    bm = ROW_BLOCK
    row = lambda w: pl.BlockSpec((bm, w), lambda i: (i, 0))
    full = lambda a: pl.BlockSpec(a.shape, lambda i: (0,) * a.ndim)
    bf16, f32 = jnp.bfloat16, jnp.float32
    outs = [(FOX_W, bf16), (FOX_W, bf16), (FOX_W, bf16), (FOX_W, f32),
            (POOL_W, f32), (POOL_W, f32),
            (SB_W, bf16), (SB_W, bf16), (SB_W, bf16), (SB_W, f32),
            (FOX_HEADS, f32)]
    return pl.pallas_call(
        _in_proj_kernel,
        grid=(m // bm,),
        in_specs=[row(D_MODEL), full(norm_g), full(w_all), full(hsum),
                  full(q_g), full(k_g), full(b_f)],
        out_specs=[row(w) for w, _ in outs],
        out_shape=[jax.ShapeDtypeStruct((m, w), dt) for w, dt in outs],
        compiler_params=pltpu.CompilerParams(
            dimension_semantics=("arbitrary",), vmem_limit_bytes=VMEM_LIMIT),
    )(x2d, norm_g, w_all, hsum, q_g, k_g, b_f)


def _scan_kernel(lf_ref, tri_ref, prev_ref, c_ref):
    lf = lf_ref[0]
    tri = tri_ref[...]
    within = sum(_dot(p, tri) for p in _bf16_split3(lf))
    total = jnp.broadcast_to(within[:, SCAN_CHUNK - 1:SCAN_CHUNK], within.shape)
    prev = prev_ref[...]
    offset = sum(_dot(prev, p) for p in _bf16_split3(total))
    c_ref[0] = within + offset


def _forget_scan(lf_rows, tri, prev):
    b, r, c = lf_rows.shape
    full = lambda a: pl.BlockSpec(a.shape, lambda i: (0,) * a.ndim)
    blk = pl.BlockSpec((1, r, c), lambda i: (i, 0, 0))
    return pl.pallas_call(
        _scan_kernel,
        grid=(b,),
        in_specs=[blk, full(tri), full(prev)],
        out_specs=blk,
        out_shape=jax.ShapeDtypeStruct((b, r, c), jnp.float32),
        compiler_params=pltpu.CompilerParams(dimension_semantics=("arbitrary",)),
    )(lf_rows, tri, prev)


def _lane_is_first_head(shape):
    return lax.broadcasted_iota(jnp.int32, shape, len(shape) - 1) < HEAD_DIM


def _split_heads(q):
    first = _lane_is_first_head(q.shape)
    zero = jnp.zeros_like(q)
    return jnp.where(first, q, zero), jnp.where(first, zero, q)


def _widen(col, width):
    return jnp.concatenate([col] * (width // LANES), axis=1)


def _fox_kernel(bound_ref, *refs):
    bounded = bound_ref[0, 0] <= FOX_MAX_BOUND

    @pl.when(bounded)
    def _():
        _fox_path(bound_ref, *refs, running_max=False)

    @pl.when(jnp.logical_not(bounded))
    def _():
        _fox_path(bound_ref, *refs, running_max=True)


def _fox_path(bound_ref, q_ref, k_ref, v_ref, ccol_ref, crow_ref, g_ref, o_ref,
              qs_ref, m_ref, acc_ref, ct_ref, *, running_max):
    bq, bk = ATT_BQ, ATT_BK
    qi = pl.program_id(2)
    q0, q1 = _split_heads(q_ref[0])
    qs_ref[0] = q0
    qs_ref[1] = q1
    shift = 0.0 if running_max else bound_ref[0, 0]
    for h in range(2):
        m_ref[h] = jnp.full((bq, LANES), NEG, jnp.float32)
        acc_ref[h] = jnp.zeros((bq, PAIR_W), jnp.float32)
        ct_ref[h] = jnp.broadcast_to(ccol_ref[0, 0, :, h:h + 1], (bq, LANES)) - shift

    def run(blocks, masked):
        chains = []
        for j, bk, r0 in blocks:
            n = bq - r0
            start = pl.multiple_of(j * bk, bk)
            k = k_ref[0, pl.ds(start, bk), :]
            v = v_ref[0, pl.ds(start, bk), :]
            first = _lane_is_first_head(v.shape)
            ones = jnp.ones_like(v)
            vs = (jnp.where(first, v, ones), jnp.where(first, ones, v))
            keep = None
            if masked:
                keep = (lax.broadcasted_iota(jnp.int32, (n, bk), 1)
                        <= lax.broadcasted_iota(jnp.int32, (n, bk), 0))
            for h in range(2):
                cs = crow_ref[0, 0, h:h + 1, pl.ds(start, bk)]
                chains.append((h, slice(r0, bq), bk, k, vs[h], cs, keep))
        qk = [_dot_nt(qs_ref[h, rows, :], k) for h, rows, _, k, _, _, _ in chains]
        ps = []
        for (h, rows, bk, _, _, cs, keep), s in zip(chains, qk):
            s = s + (_widen(ct_ref[h, rows, :], bk) - cs)
            if masked:
                s = jnp.where(keep, s, NEG)
            if running_max:
                m_prev = m_ref[h, rows, :]
                m_next = jnp.maximum(m_prev, jnp.max(s, axis=1, keepdims=True))
                s = s - _widen(m_next, bk)
                acc_ref[h, rows, :] = jnp.exp(m_prev - m_next) * acc_ref[h, rows, :]
                m_ref[h, rows, :] = m_next
            ps.append(jnp.exp(s).astype(jnp.bfloat16))
        for (h, rows, _, _, v, _, _), p in zip(chains, ps):
            acc_ref[h, rows, :] = acc_ref[h, rows, :] + _dot(p, v)

    per_trip = 1 if running_max else FOX_BLOCKS_PER_TRIP
    assert (bq // FOX_BK_FULL) % per_trip == 0

    def body(j, carry):
        run([(j * per_trip + u, FOX_BK_FULL, 0) for u in range(per_trip)], False)
        return carry

    lax.fori_loop(0, qi * (bq // FOX_BK_FULL // per_trip), body, 0)
    sub = bq // bk
    diagonal = [(qi * sub + jj, bk, jj * bk) for jj in range(sub)]
    if running_max:
        for blk in diagonal:
            run([blk], True)
    else:
        run(diagonal, True)

    a0, a1 = acc_ref[0], acc_ref[1]
    o0 = a0 / pltpu.roll(a0, HEAD_DIM, axis=1)
    o1 = a1 / pltpu.roll(a1, HEAD_DIM, axis=1)
    o = jnp.where(_lane_is_first_head(o0.shape), o0, o1)
    o_ref[0] = (o * _silu(g_ref[0])).astype(o_ref.dtype)


def _fox_attention(bound, q, k, v, ccol, crow, gate):
    b, s, w = q.shape
    t = ATT_BQ
    pairs = w // PAIR_W
    qblk = pl.BlockSpec((1, t, PAIR_W), lambda bi, hp, qi: (bi, qi, hp))
    seq = pl.BlockSpec((1, s, PAIR_W), lambda bi, hp, qi: (bi, 0, hp))
    return pl.pallas_call(
        _fox_kernel,
        grid=(b, pairs, s // t),
        in_specs=[pl.BlockSpec(memory_space=pltpu.SMEM), qblk, seq, seq,
                  pl.BlockSpec((1, 1, t, 2), lambda bi, hp, qi: (bi, hp, qi, 0)),
                  pl.BlockSpec((1, 1, 2, s), lambda bi, hp, qi: (bi, hp, 0, 0)),
                  qblk],
        out_specs=qblk,
        out_shape=jax.ShapeDtypeStruct((b, s, w), jnp.bfloat16),
        scratch_shapes=[pltpu.VMEM((2, t, PAIR_W), jnp.bfloat16),
                        pltpu.VMEM((2, t, LANES), jnp.float32),
                        pltpu.VMEM((2, t, PAIR_W), jnp.float32),
                        pltpu.VMEM((2, t, LANES), jnp.float32)],
        compiler_params=pltpu.CompilerParams(
            dimension_semantics=("arbitrary", "arbitrary", "arbitrary"),
            vmem_limit_bytes=VMEM_LIMIT),
    )(bound, q, k, v, ccol, crow, gate)


def _fox_logit_bound(q_gain, k_gain):
    gq = jnp.max(jnp.abs(q_gain))
    gk = jnp.max(jnp.abs(k_gain))
    return (HEAD_DIM * SCALE * 1.01) * gq * gk + 0.05


def _sb_kernel(q_ref, k_ref, v_ref, g_ref, tri_ref, o_ref, qs_ref, r_ref, acc_ref):
    bq, t = ATT_BQ, ATT_BK
    sub = bq // t
    qi = pl.program_id(2)
    q0, q1 = _split_heads(q_ref[0])
    qs_ref[0] = q0
    qs_ref[1] = q1
    for h in range(2):
        r_ref[h] = jnp.zeros((bq, LANES), jnp.float32)
        acc_ref[h] = jnp.zeros((bq, PAIR_W), jnp.float32)

    def step(d, diagonal):
        tri = tri_ref[...]
        if diagonal:
            keep = (lax.broadcasted_iota(jnp.int32, (t, t), 1)
                    < lax.broadcasted_iota(jnp.int32, (t, t), 0))
        chains = [(a, h) for a in range(sub) for h in range(2)]
        rows = {a: slice(a * t, (a + 1) * t) for a in range(sub)}
        valid, ks, vs = {}, {}, {}
        for a in range(sub):
            j = qi * sub + a - d
            valid[a] = j >= 0
            start = pl.multiple_of(jnp.maximum(j, 0) * t, t)
            ks[a] = k_ref[0, pl.ds(start, t), :]
            vs[a] = v_ref[0, pl.ds(start, t), :]
        z = {(a, h): _dot_nt(qs_ref[h, rows[a], :], ks[a]) for a, h in chains}
        parts = {}
        for c in chains:
            sp = _softplus(z[c])
            if diagonal:
                sp = jnp.where(keep, sp, 0.0)
            parts[c] = _bf16_split2(sp)
        cum = {c: _dot(parts[c][0], tri) + _dot(parts[c][1], tri) for c in chains}
        w = {}
        for c in chains:
            wc = jnp.exp(z[c] - cum[c])
            if diagonal:
                wc = jnp.where(keep, wc, 0.0)
            w[c] = wc.astype(jnp.bfloat16)
        for a, h in chains:
            r_prev = r_ref[h, rows[a], :]
            pv = jnp.exp(-r_prev) * _dot(w[a, h], vs[a])
            mass = jnp.broadcast_to(cum[a, h][:, 0:1], (t, LANES))
            if not diagonal:
                pv = jnp.where(valid[a], pv, 0.0)
                mass = jnp.where(valid[a], mass, 0.0)
            acc_ref[h, rows[a], :] = acc_ref[h, rows[a], :] + pv
            r_ref[h, rows[a], :] = r_prev + mass

    def finished(d):
        done = jnp.bool_(True)
        for a in range(sub):
            rows = slice(a * t, (a + 1) * t)
            exhausted = qi * sub + a - (d + 1) < 0
            saturated = jnp.min(r_ref[:, rows, :]) >= SB_CUTOFF
            done = jnp.logical_and(done, jnp.logical_or(exhausted, saturated))
        return done

    step(0, True)

    def cond(carry):
        _, done = carry
        return jnp.logical_not(done)

    def body(carry):
        d, _ = carry
        step(d, False)
        return d + 1, finished(d)

    lax.while_loop(cond, body, (jnp.int32(1), finished(0)))

    o = jnp.where(_lane_is_first_head((bq, PAIR_W)), acc_ref[0], acc_ref[1])
    o_ref[0] = (o * _silu(g_ref[0])).astype(o_ref.dtype)


def _sb_attention(q, k, v, gate, tri):
    b, s, w = q.shape
    t = ATT_BQ
    pairs = w // PAIR_W
    qblk = pl.BlockSpec((1, t, PAIR_W), lambda bi, hp, qi: (bi, qi, hp))
    seq = pl.BlockSpec((1, s, PAIR_W), lambda bi, hp, qi: (bi, 0, hp))
    return pl.pallas_call(
        _sb_kernel,
        grid=(b, pairs, s // t),
        in_specs=[qblk, seq, seq, qblk,
                  pl.BlockSpec(tri.shape, lambda bi, hp, qi: (0, 0))],
        out_specs=qblk,
        out_shape=jax.ShapeDtypeStruct((b, s, w), jnp.bfloat16),
        scratch_shapes=[pltpu.VMEM((2, t, PAIR_W), jnp.bfloat16),
                        pltpu.VMEM((2, t, LANES), jnp.float32),
                        pltpu.VMEM((2, t, PAIR_W), jnp.float32)],
        compiler_params=pltpu.CompilerParams(
            dimension_semantics=("arbitrary", "arbitrary", "arbitrary"),
            vmem_limit_bytes=VMEM_LIMIT),
    )(q, k, v, gate, tri)


def _out_proj_kernel(x_ref, fox_ref, sb_ref, px_ref, hist_ref, pg_ref,
                     wpool_ref, pscale_ref, wo_ref, o_ref, xp_ref, *, blocks_per_seq):
    bm = ROW_BLOCK
    i = pl.program_id(0)
    pos0 = (i % blocks_per_seq) * bm
    hist = hist_ref[...]
    xp_ref[0:MAX_WINDOW, :] = jnp.where(pos0 == 0, jnp.zeros_like(hist), hist)
    px = px_ref[...]
    xp_ref[MAX_WINDOW:MAX_WINDOW + bm, :] = px

    sums = {}
    run = px
    for d in range(1, MAX_WINDOW):
        run = run + xp_ref[MAX_WINDOW - d:MAX_WINDOW - d + bm, :]
        if d + 1 in POOL_WINDOWS:
            sums[d + 1] = run
    group = lax.broadcasted_iota(jnp.int32, (bm, POOL_W), 1) // POOL_GROUP_DIM
    pos = pos0 + lax.broadcasted_iota(jnp.int32, (bm, POOL_W), 0)
    wsum = sums[POOL_WINDOWS[-1]]
    win = jnp.full((bm, POOL_W), POOL_WINDOWS[-1], jnp.int32)
    for g in range(POOL_GROUPS - 2, -1, -1):
        wsum = jnp.where(group == g, sums[POOL_WINDOWS[g]], wsum)
        win = jnp.where(group == g, POOL_WINDOWS[g], win)
    count = jnp.minimum(pos + 1, win).astype(jnp.float32)
    pooled = wsum / count - px
    y = _dot(pooled.astype(jnp.bfloat16), wpool_ref[...]) * pscale_ref[...]
    pool_out = (y * _silu(pg_ref[...])).astype(jnp.bfloat16)

    o_ref[...] = (x_ref[...]
                  + _dot(fox_ref[...], wo_ref[0:FOX_W, :])
                  + _dot(pool_out, wo_ref[FOX_W:FOX_W + POOL_W, :])
                  + _dot(sb_ref[...], wo_ref[FOX_W + POOL_W:D_MIX, :]))


def _out_proj(x2d, fox_o, sb_o, px, pg, wpool_bd, pscale, w_out, seq_len):
    m = x2d.shape[0]
    bm = ROW_BLOCK
    row = lambda w: pl.BlockSpec((bm, w), lambda i: (i, 0))
    full = lambda a: pl.BlockSpec(a.shape, lambda i: (0,) * a.ndim)
    hist_per_block = bm // MAX_WINDOW
    hist = pl.BlockSpec((MAX_WINDOW, POOL_W),
                        lambda i: (jnp.maximum(i * hist_per_block - 1, 0), 0))
    kern = functools.partial(_out_proj_kernel, blocks_per_seq=seq_len // bm)
    return pl.pallas_call(
        kern,
        grid=(m // bm,),
        in_specs=[row(D_MODEL), row(FOX_W), row(SB_W), row(POOL_W), hist, row(POOL_W),
                  full(wpool_bd), full(pscale), full(w_out)],
        out_specs=row(D_MODEL),
        out_shape=jax.ShapeDtypeStruct((m, D_MODEL), jnp.float32),
        scratch_shapes=[pltpu.VMEM((MAX_WINDOW + bm, POOL_W), jnp.float32)],
        compiler_params=pltpu.CompilerParams(
            dimension_semantics=("arbitrary",), vmem_limit_bytes=VMEM_LIMIT),
    )(x2d, fox_o, sb_o, px, px, pg, wpool_bd, pscale, w_out)


def _constants(seq_len):
    bf16 = jnp.bfloat16
    idx = jnp.arange(MXU_TILE)
    hsum =(idx[:, None] // HEAD_DIM == idx[None, :] // HEAD_DIM).astype(bf16)
    c = jnp.arange(SCAN_CHUNK)
    scan_tri = (c[:, None] <= c[None, :]).astype(bf16)
    nc = seq_len // SCAN_CHUNK
    r = jnp.arange(FOX_HEADS * nc)
    scan_prev = ((r[:, None] // nc == r[None, :] // nc)
                 & (r[None, :] < r[:, None])).astype(bf16)
    a = jnp.arange(ATT_BK)
    sb_tri = (a[:, None] >= a[None, :]).astype(bf16)
    return hsum, scan_tri, scan_prev, sb_tri


def _pack_w_in(w):
    split = 4 * FOX_W
    main = jnp.concatenate([w[:, :split], w[:, split + FOX_HEADS:]], axis=1)
    ff = jnp.pad(w[:, split:split + FOX_HEADS], ((0, 0), (0, FF_PAD - FOX_HEADS)))
    return jnp.concatenate([main, ff], axis=1).astype(jnp.bfloat16)


def _block_diag(w_pool):
    out = jnp.zeros((POOL_W, POOL_W), w_pool.dtype)
    for g in range(POOL_GROUPS):
        lo = g * POOL_GROUP_DIM
        out = out.at[lo:lo + POOL_GROUP_DIM, lo:lo + POOL_GROUP_DIM].set(w_pool[g])
    return out.astype(jnp.bfloat16)


def kernel(x, norm_g, w_in, b_f, q_norm_g, k_norm_g, w_pool, pool_scale, w_out):
    b, s, d = x.shape
    depth = norm_g.shape[0]
    assert d == D_MODEL and s % ROW_BLOCK == 0 and s % ATT_BQ == 0 and ATT_BQ % ATT_BK == 0
    assert w_in.shape[-1] == C_FF + FOX_HEADS
    hsum, scan_tri, scan_prev, sb_tri = _constants(s)
    nc = s // SCAN_CHUNK
    x2d = x.reshape(b * s, d)
    for l in range(depth):
        w_all = _pack_w_in(w_in[l])
        q_g = jnp.tile(q_norm_g[l], FOX_HEADS)[None, :]
        k_g = jnp.tile(k_norm_g[l], FOX_HEADS)[None, :]
        (fq, fk, fv, fg, px, pg, sq, sk, sv, sg, lf) = _in_proj(
            x2d, norm_g[l][None, :], w_all, hsum, q_g, k_g, b_f[l][None, :])

        lf_rows = lf.reshape(b, s, FOX_HEADS).transpose(0, 2, 1).reshape(b, FOX_HEADS * nc, SCAN_CHUNK)
        c = _forget_scan(lf_rows, scan_tri, scan_prev)
        crow = c.reshape(b, FOX_HEADS // 2, 2, s)
        ccol = crow.transpose(0, 1, 3, 2)

        r3 = lambda a: a.reshape(b, s, a.shape[-1])
        bound = _fox_logit_bound(q_norm_g[l], k_norm_g[l]).reshape(1, 1)
        fox_o = _fox_attention(bound, r3(fq), r3(fk), r3(fv), ccol, crow, r3(fg))
        sb_o = _sb_attention(r3(sq), r3(sk), r3(sv), r3(sg), sb_tri)

        x2d = _out_proj(x2d, fox_o.reshape(b * s, FOX_W), sb_o.reshape(b * s, SB_W),
                        px, pg, _block_diag(w_pool[l]), pool_scale[l][None, :],
                        w_out[l].astype(jnp.bfloat16), s)
    return x2d.reshape(b, s, d)
```

```python
import functools

import jax
import jax.numpy as jnp
from jax import lax
from jax.experimental import pallas as pl
from jax.experimental.pallas import tpu as pltpu

D_MODEL = 1024
HEAD_DIM = 64
FOX_HEADS = 8
SB_HEADS = 4
POOL_GROUPS = 4
POOL_WINDOWS = (2, 4, 8, 16)
POOL_GROUP_DIM = 64
FOX_W = FOX_HEADS * HEAD_DIM
SB_W = SB_HEADS * HEAD_DIM
POOL_W = POOL_GROUPS * POOL_GROUP_DIM
D_MIX = FOX_W + POOL_W + SB_W
EPS = 1e-6
NEG = -1e30
SCALE = HEAD_DIM ** -0.5
LOG2E = 1.4426950408889634
FOX_MAX_BOUND = 30.0
SB_CUTOFF = 106.0
FOX_CUTOFF = 106.0

LANES = 128
MXU_TILE = 256
PAIR_W = 2 * HEAD_DIM
MAX_WINDOW = max(POOL_WINDOWS)
FF_PAD = LANES

ROW_BLOCK = 512
ATT_BQ = 1024
ATT_BK = 256
FOX_BK_FULL = 512
FOX_BLOCKS_PER_TRIP = 2
SCAN_CHUNK = LANES

VMEM_LIMIT = 56 * 1024 * 1024

C_FQ, C_FK, C_FV, C_FG = 0, FOX_W, 2 * FOX_W, 3 * FOX_W
C_PX = 4 * FOX_W
C_PG = C_PX + POOL_W
C_SQ = C_PG + POOL_W
C_SK = C_SQ + SB_W
C_SV = C_SK + SB_W
C_SG = C_SV + SB_W
C_FF = C_SG + SB_W
D_IN_PAD = C_FF + FF_PAD


def _bf16_split2(x):
    hi = x.astype(jnp.bfloat16)
    lo = (x - hi.astype(jnp.float32)).astype(jnp.bfloat16)
    return hi, lo


def _bf16_split3(x):
    a = x.astype(jnp.bfloat16)
    r = x - a.astype(jnp.float32)
    b = r.astype(jnp.bfloat16)
    c = (r - b.astype(jnp.float32)).astype(jnp.bfloat16)
    return a, b, c


def _dot(a, b):
    return jnp.dot(a, b, preferred_element_type=jnp.float32)


def _dot_nt(a, b):
    return lax.dot_general(a, b, (((1,), (1,)), ((), ())),
                           preferred_element_type=jnp.float32)


def _silu(x):
    return x * (1.0 / (1.0 + jnp.exp(-x)))


def _softplus(x):
    return jnp.maximum(x, 0.0) + jnp.log(1.0 + jnp.exp2(jnp.abs(x) * (-LOG2E)))


def _in_proj_kernel(x_ref, g_ref, w_ref, hsum_ref, qg_ref, kg_ref, bf_ref,
                    fq_ref, fk_ref, fv_ref, fg_ref, px_ref, pg_ref,
                    sq_ref, sk_ref, sv_ref, sg_ref, lf_ref):
    x = x_ref[...]
    ms = jnp.mean(x * x, axis=-1, keepdims=True)
    h = (x * lax.rsqrt(ms + EPS) * g_ref[...]).astype(jnp.bfloat16)

    def proj(c0, width):
        return _dot(h, w_ref[:, c0:c0 + width])

    def head_norm(y, gain):
        hi, lo = _bf16_split2(y * y)
        hsum = hsum_ref[...]
        w = hsum.shape[0]
        ssq = jnp.concatenate(
            [_dot(hi[:, c:c + w], hsum) + _dot(lo[:, c:c + w], hsum)
             for c in range(0, y.shape[1], w)], axis=1)
        return y * lax.rsqrt(ssq * (1.0 / HEAD_DIM) + EPS) * gain

    fq_ref[...] = (head_norm(proj(C_FQ, FOX_W), qg_ref[...]) * SCALE).astype(jnp.bfloat16)
    fk_ref[...] = head_norm(proj(C_FK, FOX_W), kg_ref[...]).astype(jnp.bfloat16)
    fv_ref[...] = proj(C_FV, FOX_W).astype(jnp.bfloat16)
    fg_ref[...] = proj(C_FG, FOX_W)
    px_ref[...] = proj(C_PX, POOL_W)
    pg_ref[...] = proj(C_PG, POOL_W)
    sq_ref[...] = (proj(C_SQ, SB_W) * SCALE).astype(jnp.bfloat16)
    sk_ref[...] = proj(C_SK, SB_W).astype(jnp.bfloat16)
    sv_ref[...] = proj(C_SV, SB_W).astype(jnp.bfloat16)
    sg_ref[...] = proj(C_SG, SB_W)
    ff = proj(C_FF, FF_PAD)[:, :FOX_HEADS] + bf_ref[...]
    lf_ref[...] = -_softplus(-ff)


def _layer_spec(stacked, layer):
    tail = stacked.shape[1:]
    return pl.BlockSpec((None,) + tail, lambda i: (layer,) + (0,) * len(tail))


def _in_proj(layer, x2d, norm_g, w_all, hsum, q_g, k_g, b_f):
    m = x2d.shape[0]
    bm = ROW_BLOCK
    row = lambda w: pl.BlockSpec((bm, w), lambda i: (i, 0))
    full = lambda a: pl.BlockSpec(a.shape, lambda i: (0,) * a.ndim)
    per_layer = lambda a: _layer_spec(a, layer)
    bf16, f32 = jnp.bfloat16, jnp.float32
    outs = [(FOX_W, bf16), (FOX_W, bf16), (FOX_W, bf16), (FOX_W, f32),
            (POOL_W, f32), (POOL_W, f32),
            (SB_W, bf16), (SB_W, bf16), (SB_W, bf16), (SB_W, f32),
            (FOX_HEADS, f32)]
    return pl.pallas_call(
        _in_proj_kernel,
        grid=(m // bm,),
        in_specs=[row(D_MODEL), per_layer(norm_g), per_layer(w_all), full(hsum),
                  per_layer(q_g), per_layer(k_g), per_layer(b_f)],
        out_specs=[row(w) for w, _ in outs],
        out_shape=[jax.ShapeDtypeStruct((m, w), dt) for w, dt in outs],
        compiler_params=pltpu.CompilerParams(
            dimension_semantics=("arbitrary",), vmem_limit_bytes=VMEM_LIMIT),
    )(x2d, norm_g, w_all, hsum, q_g, k_g, b_f)


def _scan_kernel(lf_ref, tri_ref, prev_ref, c_ref):
    lf = lf_ref[0]
    tri = tri_ref[...]
    within = sum(_dot(p, tri) for p in _bf16_split3(lf))
    total = jnp.broadcast_to(within[:, SCAN_CHUNK - 1:SCAN_CHUNK], within.shape)
    prev = prev_ref[...]
    offset = sum(_dot(prev, p) for p in _bf16_split3(total))
    c_ref[0] = within + offset


def _forget_scan(lf_rows, tri, prev):
    b, r, c = lf_rows.shape
    full = lambda a: pl.BlockSpec(a.shape, lambda i: (0,) * a.ndim)
    blk = pl.BlockSpec((1, r, c), lambda i: (i, 0, 0))
    return pl.pallas_call(
        _scan_kernel,
        grid=(b,),
        in_specs=[blk, full(tri), full(prev)],
        out_specs=blk,
        out_shape=jax.ShapeDtypeStruct((b, r, c), jnp.float32),
        compiler_params=pltpu.CompilerParams(dimension_semantics=("arbitrary",)),
    )(lf_rows, tri, prev)


def _lane_is_first_head(shape):
    return lax.broadcasted_iota(jnp.int32, shape, len(shape) - 1) < HEAD_DIM


def _split_heads(q):
    first = _lane_is_first_head(q.shape)
    zero = jnp.zeros_like(q)
    return jnp.where(first, q, zero), jnp.where(first, zero, q)


def _widen(col, width):
    return jnp.concatenate([col] * (width // LANES), axis=1)


def _fox_kernel(bound_ref, *refs):
    bounded = bound_ref[0, 0] <= FOX_MAX_BOUND

    @pl.when(bounded)
    def _():
        _fox_path(bound_ref, *refs, running_max=False)

    @pl.when(jnp.logical_not(bounded))
    def _():
        _fox_path(bound_ref, *refs, running_max=True)


def _fox_path(bound_ref, q_ref, k_ref, v_ref, ccol_ref, crow_ref, g_ref, o_ref,
              qs_ref, m_ref, acc_ref, ct_ref, *, running_max):
    bq, bk = ATT_BQ, ATT_BK
    qi = pl.program_id(2)
    q0, q1 = _split_heads(q_ref[0])
    qs_ref[0] = q0
    qs_ref[1] = q1
    shift = 0.0 if running_max else bound_ref[0, 0]
    for h in range(2):
        m_ref[h] = jnp.full((bq, LANES), NEG, jnp.float32)
        acc_ref[h] = jnp.zeros((bq, PAIR_W), jnp.float32)
        ct_ref[h] = jnp.broadcast_to(ccol_ref[0, 0, :, h:h + 1], (bq, LANES)) - shift

    def run(blocks, masked, heads=(0, 1)):
        chains = []
        for j, bk, r0 in blocks:
            n = bq - r0
            start = pl.multiple_of(j * bk, bk)
            k = k_ref[0, pl.ds(start, bk), :]
            v = v_ref[0, pl.ds(start, bk), :]
            first = _lane_is_first_head(v.shape)
            ones = jnp.ones_like(v)
            vs = (jnp.where(first, v, ones), jnp.where(first, ones, v))
            keep = None
            if masked:
                keep = (lax.broadcasted_iota(jnp.int32, (n, bk), 1)
                        <= lax.broadcasted_iota(jnp.int32, (n, bk), 0))
            for h in heads:
                cs = crow_ref[0, 0, h:h + 1, pl.ds(start, bk)]
                chains.append((h, slice(r0, bq), bk, k, vs[h], cs, keep))
        qk = [_dot_nt(qs_ref[h, rows, :], k) for h, rows, _, k, _, _, _ in chains]
        ps = []
        for (h, rows, bk, _, _, cs, keep), s in zip(chains, qk):
            s = s + (_widen(ct_ref[h, rows, :], bk) - cs)
            if masked:
                s = jnp.where(keep, s, NEG)
            if running_max:
                m_prev = m_ref[h, rows, :]
                m_next = jnp.maximum(m_prev, jnp.max(s, axis=1, keepdims=True))
                s = s - _widen(m_next, bk)
                acc_ref[h, rows, :] = jnp.exp(m_prev - m_next) * acc_ref[h, rows, :]
                m_ref[h, rows, :] = m_next
            ps.append(jnp.exp(s).astype(jnp.bfloat16))
        for (h, rows, _, _, v, _, _), p in zip(chains, ps):
            acc_ref[h, rows, :] = acc_ref[h, rows, :] + _dot(p, v)

    per_trip = 1 if running_max else FOX_BLOCKS_PER_TRIP
    assert (bq // FOX_BK_FULL) % per_trip == 0
    trips = qi * (bq // FOX_BK_FULL // per_trip)

    def trip(heads):
        def body(j, carry):
            run([(j * per_trip + u, FOX_BK_FULL, 0) for u in range(per_trip)], False, heads)
            return carry
        return body

    if running_max:
        lax.fori_loop(0, trips, trip((0, 1)), 0)
    else:
        span = FOX_BK_FULL * per_trip
        pos = lax.broadcasted_iota(jnp.int32, (1, crow_ref.shape[-1]), 1)
        first_trip = []
        for h in range(2):
            bias = ccol_ref[0, 0, 0:1, h:h + 1] - crow_ref[0, 0, h:h + 1, :]
            dead = jnp.logical_and(bias <= -FOX_CUTOFF, pos < qi * bq)
            n_dead = jnp.sum(jnp.where(dead, 1.0, 0.0)).astype(jnp.int32)
            first_trip.append(n_dead // span)
        both = jnp.maximum(first_trip[0], first_trip[1])
        lax.fori_loop(first_trip[0], both, trip((0,)), 0)
        lax.fori_loop(first_trip[1], both, trip((1,)), 0)
        lax.fori_loop(both, trips, trip((0, 1)), 0)
    sub = bq // bk
    diagonal = [(qi * sub + jj, bk, jj * bk) for jj in range(sub)]
    if running_max:
        for blk in diagonal:
            run([blk], True)
    else:
        run(diagonal, True)

    a0, a1 = acc_ref[0], acc_ref[1]
    o0 = a0 / pltpu.roll(a0, HEAD_DIM, axis=1)
    o1 = a1 / pltpu.roll(a1, HEAD_DIM, axis=1)
    o = jnp.where(_lane_is_first_head(o0.shape), o0, o1)
    o_ref[0] = (o * _silu(g_ref[0])).astype(o_ref.dtype)


def _fox_attention(bound, q, k, v, ccol, crow, gate):
    b, s, w = q.shape
    t = ATT_BQ
    pairs = w // PAIR_W
    qblk = pl.BlockSpec((1, t, PAIR_W), lambda bi, hp, qi: (bi, qi, hp))
    seq = pl.BlockSpec((1, s, PAIR_W), lambda bi, hp, qi: (bi, 0, hp))
    return pl.pallas_call(
        _fox_kernel,
        grid=(b, pairs, s // t),
        in_specs=[pl.BlockSpec(memory_space=pltpu.SMEM), qblk, seq, seq,
                  pl.BlockSpec((1, 1, t, 2), lambda bi, hp, qi: (bi, hp, qi, 0)),
                  pl.BlockSpec((1, 1, 2, s), lambda bi, hp, qi: (bi, hp, 0, 0)),
                  qblk],
        out_specs=qblk,
        out_shape=jax.ShapeDtypeStruct((b, s, w), jnp.bfloat16),
        scratch_shapes=[pltpu.VMEM((2, t, PAIR_W), jnp.bfloat16),
                        pltpu.VMEM((2, t, LANES), jnp.float32),
                        pltpu.VMEM((2, t, PAIR_W), jnp.float32),
                        pltpu.VMEM((2, t, LANES), jnp.float32)],
        compiler_params=pltpu.CompilerParams(
            dimension_semantics=("arbitrary", "arbitrary", "arbitrary"),
            vmem_limit_bytes=VMEM_LIMIT),
    )(bound, q, k, v, ccol, crow, gate)


def _fox_logit_bound(q_gain, k_gain):
    gq = jnp.max(jnp.abs(q_gain), axis=-1)
    gk = jnp.max(jnp.abs(k_gain), axis=-1)
    return (HEAD_DIM * SCALE * 1.01) * gq * gk + 0.05


def _sb_kernel(q_ref, k_ref, v_ref, g_ref, tri_ref, o_ref, qs_ref, r_ref, acc_ref):
    bq, t = ATT_BQ, ATT_BK
    sub = bq // t
    qi = pl.program_id(2)
    q0, q1 = _split_heads(q_ref[0])
    qs_ref[0] = q0
    qs_ref[1] = q1
    for h in range(2):
        r_ref[h] = jnp.zeros((bq, LANES), jnp.float32)
        acc_ref[h] = jnp.zeros((bq, PAIR_W), jnp.float32)

    def step(d, diagonal):
        tri = tri_ref[...]
        if diagonal:
            keep = (lax.broadcasted_iota(jnp.int32, (t, t), 1)
                    < lax.broadcasted_iota(jnp.int32, (t, t), 0))
        chains = [(a, h) for a in range(sub) for h in range(2)]
        rows = {a: slice(a * t, (a + 1) * t) for a in range(sub)}
        valid, ks, vs = {}, {}, {}
        for a in range(sub):
            j = qi * sub + a - d
            valid[a] = j >= 0
            start = pl.multiple_of(jnp.maximum(j, 0) * t, t)
            ks[a] = k_ref[0, pl.ds(start, t), :]
            vs[a] = v_ref[0, pl.ds(start, t), :]
        z = {(a, h): _dot_nt(qs_ref[h, rows[a], :], ks[a]) for a, h in chains}
        parts = {}
        for c in chains:
            sp = _softplus(z[c])
            if diagonal:
                sp = jnp.where(keep, sp, 0.0)
            parts[c] = _bf16_split2(sp)
        cum = {c: _dot(parts[c][0], tri) + _dot(parts[c][1], tri) for c in chains}
        w = {}
        for c in chains:
            wc = jnp.exp(z[c] - cum[c])
            if diagonal:
                wc = jnp.where(keep, wc, 0.0)
            w[c] = wc.astype(jnp.bfloat16)
        for a, h in chains:
            r_prev = r_ref[h, rows[a], :]
            pv = jnp.exp(-r_prev) * _dot(w[a, h], vs[a])
            mass = jnp.broadcast_to(cum[a, h][:, 0:1], (t, LANES))
            if not diagonal:
                pv = jnp.where(valid[a], pv, 0.0)
                mass = jnp.where(valid[a], mass, 0.0)
            acc_ref[h, rows[a], :] = acc_ref[h, rows[a], :] + pv
            r_ref[h, rows[a], :] = r_prev + mass

    def finished(d):
        done = jnp.bool_(True)
        for a in range(sub):
            rows = slice(a * t, (a + 1) * t)
            exhausted = qi * sub + a - (d + 1) < 0
            saturated = jnp.min(r_ref[:, rows, :]) >= SB_CUTOFF
            done = jnp.logical_and(done, jnp.logical_or(exhausted, saturated))
        return done

    step(0, True)

    def cond(carry):
        _, done = carry
        return jnp.logical_not(done)

    def body(carry):
        d, _ = carry
        step(d, False)
        return d + 1, finished(d)

    lax.while_loop(cond, body, (jnp.int32(1), finished(0)))

    o = jnp.where(_lane_is_first_head((bq, PAIR_W)), acc_ref[0], acc_ref[1])
    o_ref[0] = (o * _silu(g_ref[0])).astype(o_ref.dtype)


def _sb_attention(q, k, v, gate, tri):
    b, s, w = q.shape
    t = ATT_BQ
    pairs = w // PAIR_W
    qblk = pl.BlockSpec((1, t, PAIR_W), lambda bi, hp, qi: (bi, qi, hp))
    seq = pl.BlockSpec((1, s, PAIR_W), lambda bi, hp, qi: (bi, 0, hp))
    return pl.pallas_call(
        _sb_kernel,
        grid=(b, pairs, s // t),
        in_specs=[qblk, seq, seq, qblk,
                  pl.BlockSpec(tri.shape, lambda bi, hp, qi: (0, 0))],
        out_specs=qblk,
        out_shape=jax.ShapeDtypeStruct((b, s, w), jnp.bfloat16),
        scratch_shapes=[pltpu.VMEM((2, t, PAIR_W), jnp.bfloat16),
                        pltpu.VMEM((2, t, LANES), jnp.float32),
                        pltpu.VMEM((2, t, PAIR_W), jnp.float32)],
        compiler_params=pltpu.CompilerParams(
            dimension_semantics=("arbitrary", "arbitrary", "arbitrary"),
            vmem_limit_bytes=VMEM_LIMIT),
    )(q, k, v, gate, tri)


def _out_proj_kernel(x_ref, fox_ref, sb_ref, px_ref, hist_ref, pg_ref,
                     wpool_ref, pscale_ref, wo_ref, o_ref, xp_ref, *, blocks_per_seq):
    bm = ROW_BLOCK
    i = pl.program_id(0)
    pos0 = (i % blocks_per_seq) * bm
    hist = hist_ref[...]
    xp_ref[0:MAX_WINDOW, :] = jnp.where(pos0 == 0, jnp.zeros_like(hist), hist)
    px = px_ref[...]
    xp_ref[MAX_WINDOW:MAX_WINDOW + bm, :] = px

    sums = {}
    run = px
    for d in range(1, MAX_WINDOW):
        run = run + xp_ref[MAX_WINDOW - d:MAX_WINDOW - d + bm, :]
        if d + 1 in POOL_WINDOWS:
            sums[d + 1] = run
    group = lax.broadcasted_iota(jnp.int32, (bm, POOL_W), 1) // POOL_GROUP_DIM
    pos = pos0 + lax.broadcasted_iota(jnp.int32, (bm, POOL_W), 0)
    wsum = sums[POOL_WINDOWS[-1]]
    win = jnp.full((bm, POOL_W), POOL_WINDOWS[-1], jnp.int32)
    for g in range(POOL_GROUPS - 2, -1, -1):
        wsum = jnp.where(group == g, sums[POOL_WINDOWS[g]], wsum)
        win = jnp.where(group == g, POOL_WINDOWS[g], win)
    count = jnp.minimum(pos + 1, win).astype(jnp.float32)
    pooled = wsum / count - px
    y = _dot(pooled.astype(jnp.bfloat16), wpool_ref[...]) * pscale_ref[...]
    pool_out = (y * _silu(pg_ref[...])).astype(jnp.bfloat16)

    o_ref[...] = (x_ref[...]
                  + _dot(fox_ref[...], wo_ref[0:FOX_W, :])
                  + _dot(pool_out, wo_ref[FOX_W:FOX_W + POOL_W, :])
                  + _dot(sb_ref[...], wo_ref[FOX_W + POOL_W:D_MIX, :]))


def _out_proj(layer, x2d, fox_o, sb_o, px, pg, wpool_bd, pscale, w_out, seq_len):
    m = x2d.shape[0]
    bm = ROW_BLOCK
    row = lambda w: pl.BlockSpec((bm, w), lambda i: (i, 0))
    per_layer = lambda a: _layer_spec(a, layer)
    hist_per_block = bm // MAX_WINDOW
    hist = pl.BlockSpec((MAX_WINDOW, POOL_W),
                        lambda i: (jnp.maximum(i * hist_per_block - 1, 0), 0))
    kern = functools.partial(_out_proj_kernel, blocks_per_seq=seq_len // bm)
    return pl.pallas_call(
        kern,
        grid=(m // bm,),
        in_specs=[row(D_MODEL), row(FOX_W), row(SB_W), row(POOL_W), hist, row(POOL_W),
                  per_layer(wpool_bd), per_layer(pscale), per_layer(w_out)],
        out_specs=row(D_MODEL),
        out_shape=jax.ShapeDtypeStruct((m, D_MODEL), jnp.float32),
        scratch_shapes=[pltpu.VMEM((MAX_WINDOW + bm, POOL_W), jnp.float32)],
        compiler_params=pltpu.CompilerParams(
            dimension_semantics=("arbitrary",), vmem_limit_bytes=VMEM_LIMIT),
    )(x2d, fox_o, sb_o, px, px, pg, wpool_bd, pscale, w_out)


def _constants(seq_len):
    bf16 = jnp.bfloat16
    idx = jnp.arange(MXU_TILE)
    hsum =(idx[:, None] // HEAD_DIM == idx[None, :] // HEAD_DIM).astype(bf16)
    c = jnp.arange(SCAN_CHUNK)
    scan_tri = (c[:, None] <= c[None, :]).astype(bf16)
    nc = seq_len // SCAN_CHUNK
    r = jnp.arange(FOX_HEADS * nc)
    scan_prev = ((r[:, None] // nc == r[None, :] // nc)
                 & (r[None, :] < r[:, None])).astype(bf16)
    a = jnp.arange(ATT_BK)
    sb_tri = (a[:, None] >= a[None, :]).astype(bf16)
    return hsum, scan_tri, scan_prev, sb_tri


def _pack_w_in(w):
    split = 4 * FOX_W
    main = jnp.concatenate([w[..., :split], w[..., split + FOX_HEADS:]], axis=-1)
    ff = jnp.pad(w[..., split:split + FOX_HEADS], ((0, 0), (0, 0), (0, FF_PAD - FOX_HEADS)))
    return jnp.concatenate([main, ff], axis=-1).astype(jnp.bfloat16)


def _block_diag(w_pool):
    out = jnp.zeros((w_pool.shape[0], POOL_W, POOL_W), w_pool.dtype)
    for g in range(POOL_GROUPS):
        lo = g * POOL_GROUP_DIM
        out = out.at[:, lo:lo + POOL_GROUP_DIM, lo:lo + POOL_GROUP_DIM].set(w_pool[:, g])
    return out.astype(jnp.bfloat16)


def kernel(x, norm_g, w_in, b_f, q_norm_g, k_norm_g, w_pool, pool_scale, w_out):
    b, s, d = x.shape
    depth = norm_g.shape[0]
    assert d == D_MODEL and s % ROW_BLOCK == 0 and s % ATT_BQ == 0 and ATT_BQ % ATT_BK == 0
    assert w_in.shape[-1] == C_FF + FOX_HEADS
    hsum, scan_tri, scan_prev, sb_tri = _constants(s)
    nc = s // SCAN_CHUNK
    x2d = x.reshape(b * s, d)
    w_all = _pack_w_in(w_in)
    w_out_bf = w_out.astype(jnp.bfloat16)
    wpool_bd = _block_diag(w_pool)
    rowvec = lambda a: a[:, None, :]
    norm_g3, b_f3, pscale3 = rowvec(norm_g), rowvec(b_f), rowvec(pool_scale)
    q_g = rowvec(jnp.tile(q_norm_g, (1, FOX_HEADS)))
    k_g = rowvec(jnp.tile(k_norm_g, (1, FOX_HEADS)))
    bounds = _fox_logit_bound(q_norm_g, k_norm_g)[:, None]
    for l in range(depth):
        (fq, fk, fv, fg, px, pg, sq, sk, sv, sg, lf) = _in_proj(
            l, x2d, norm_g3, w_all, hsum, q_g, k_g, b_f3)

        lf_rows = lf.reshape(b, s, FOX_HEADS).transpose(0, 2, 1).reshape(b, FOX_HEADS * nc, SCAN_CHUNK)
        c = _forget_scan(lf_rows, scan_tri, scan_prev)
        crow = c.reshape(b, FOX_HEADS // 2, 2, s)
        ccol = crow.transpose(0, 1, 3, 2)

        r3 = lambda a: a.reshape(b, s, a.shape[-1])
        fox_o = _fox_attention(bounds[l:l + 1], r3(fq), r3(fk), r3(fv), ccol, crow, r3(fg))
        sb_o = _sb_attention(r3(sq), r3(sk), r3(sv), r3(sg), sb_tri)

        x2d = _out_proj(l, x2d, fox_o.reshape(b * s, FOX_W), sb_o.reshape(b * s, SB_W),
                        px, pg, wpool_bd, pscale3, w_out_bf, s)
    return x2d.reshape(b, s, d)
```

```python
import functools

import jax
import jax.numpy as jnp
from jax import lax
from jax.experimental import pallas as pl
from jax.experimental.pallas import tpu as pltpu

D_MODEL = 1024
HEAD_DIM = 64
FOX_HEADS = 8
SB_HEADS = 4
POOL_GROUPS = 4
POOL_WINDOWS = (2, 4, 8, 16)
POOL_GROUP_DIM = 64
FOX_W = FOX_HEADS * HEAD_DIM
SB_W = SB_HEADS * HEAD_DIM
POOL_W = POOL_GROUPS * POOL_GROUP_DIM
D_MIX = FOX_W + POOL_W + SB_W
EPS = 1e-6
NEG = -1e30
SCALE = HEAD_DIM ** -0.5
LOG2E = 1.4426950408889634
FOX_MAX_BOUND = 30.0
SB_CUTOFF = 106.0
FOX_CUTOFF = 106.0

LANES = 128
MXU_TILE = 256
PAIR_W = 2 * HEAD_DIM
MAX_WINDOW = max(POOL_WINDOWS)
FF_PAD = LANES

ROW_BLOCK = 512
WEIGHT_PACK_CHUNKS = 8
ATT_BQ = 1024
ATT_BK = 256
FOX_BK_FULL = 512
FOX_BLOCKS_PER_TRIP = 2
SCAN_CHUNK = LANES

VMEM_LIMIT = 56 * 1024 * 1024

C_FQ, C_FK, C_FV, C_FG = 0, FOX_W, 2 * FOX_W, 3 * FOX_W
C_PX = 4 * FOX_W
C_PG = C_PX + POOL_W
C_SQ = C_PG + POOL_W
C_SK = C_SQ + SB_W
C_SV = C_SK + SB_W
C_SG = C_SV + SB_W
C_FF = C_SG + SB_W
D_IN_PAD = C_FF + FF_PAD


def _bf16_split2(x):
    hi = x.astype(jnp.bfloat16)
    lo = (x - hi.astype(jnp.float32)).astype(jnp.bfloat16)
    return hi, lo


def _bf16_split3(x):
    a = x.astype(jnp.bfloat16)
    r = x - a.astype(jnp.float32)
    b = r.astype(jnp.bfloat16)
    c = (r - b.astype(jnp.float32)).astype(jnp.bfloat16)
    return a, b, c


def _dot(a, b):
    return jnp.dot(a, b, preferred_element_type=jnp.float32)


def _dot_nt(a, b):
    return lax.dot_general(a, b, (((1,), (1,)), ((), ())),
                           preferred_element_type=jnp.float32)


def _silu(x):
    return x * (1.0 / (1.0 + jnp.exp(-x)))


def _softplus(x):
    return jnp.maximum(x, 0.0) + jnp.log(1.0 + jnp.exp2(jnp.abs(x) * (-LOG2E)))


def _in_proj_kernel(x_ref, g_ref, win_ref, hsum_ref, qg_ref, kg_ref, bf_ref,
                    fq_ref, fk_ref, fv_ref, fg_ref, px_ref, pg_ref,
                    sq_ref, sk_ref, sv_ref, sg_ref, lf_ref, w_ref):
    @pl.when(pl.program_id(0) == 0)
    def _():
        chunk = D_MODEL // WEIGHT_PACK_CHUNKS
        for r in range(0, D_MODEL, chunk):
            rows = slice(r, r + chunk)
            w_ref[rows, 0:C_PX] = win_ref[rows, 0:C_PX].astype(jnp.bfloat16)
            w_ref[rows, C_PX:C_FF] = win_ref[rows, C_PX + FOX_HEADS:].astype(jnp.bfloat16)
            ff = win_ref[rows, C_PX:C_PX + FOX_HEADS].astype(jnp.bfloat16)
            w_ref[rows, C_FF:] = jnp.concatenate(
                [ff, jnp.zeros((chunk, FF_PAD - FOX_HEADS), jnp.bfloat16)], axis=1)

    x = x_ref[...]
    ms = jnp.mean(x * x, axis=-1, keepdims=True)
    h = (x * lax.rsqrt(ms + EPS) * g_ref[...]).astype(jnp.bfloat16)

    def proj(c0, width):
        return _dot(h, w_ref[:, c0:c0 + width])

    def head_norm(y, gain):
        hi, lo = _bf16_split2(y * y)
        hsum = hsum_ref[...]
        w = hsum.shape[0]
        ssq = jnp.concatenate(
            [_dot(hi[:, c:c + w], hsum) + _dot(lo[:, c:c + w], hsum)
             for c in range(0, y.shape[1], w)], axis=1)
        return y * lax.rsqrt(ssq * (1.0 / HEAD_DIM) + EPS) * gain

    fq_ref[...] = (head_norm(proj(C_FQ, FOX_W), qg_ref[...]) * SCALE).astype(jnp.bfloat16)
    fk_ref[...] = head_norm(proj(C_FK, FOX_W), kg_ref[...]).astype(jnp.bfloat16)
    fv_ref[...] = proj(C_FV, FOX_W).astype(jnp.bfloat16)
    fg_ref[...] = proj(C_FG, FOX_W)
    px_ref[...] = proj(C_PX, POOL_W)
    pg_ref[...] = proj(C_PG, POOL_W)
    sq_ref[...] = (proj(C_SQ, SB_W) * SCALE).astype(jnp.bfloat16)
    sk_ref[...] = proj(C_SK, SB_W).astype(jnp.bfloat16)
    sv_ref[...] = proj(C_SV, SB_W).astype(jnp.bfloat16)
    sg_ref[...] = proj(C_SG, SB_W)
    ff = proj(C_FF, FF_PAD) + bf_ref[...]
    lf_ref[...] = (-_softplus(-ff)).T[:FOX_HEADS, :]


def _layer_spec(stacked, layer, single_buffer=False):
    tail = stacked.shape[1:]
    mode = dict(pipeline_mode=pl.Buffered(1)) if single_buffer else {}
    return pl.BlockSpec((None,) + tail, lambda i: (layer,) + (0,) * len(tail), **mode)


def _in_proj(layer, x2d, norm_g, w_in, hsum, q_g, k_g, b_f):
    m = x2d.shape[0]
    bm = ROW_BLOCK
    row = lambda w: pl.BlockSpec((bm, w), lambda i: (i, 0))
    full = lambda a: pl.BlockSpec(a.shape, lambda i: (0,) * a.ndim)
    per_layer = lambda a: _layer_spec(a, layer)
    bf16, f32 = jnp.bfloat16, jnp.float32
    outs = [(FOX_W, bf16), (FOX_W, bf16), (FOX_W, bf16), (FOX_W, f32),
            (POOL_W, f32), (POOL_W, f32),
            (SB_W, bf16), (SB_W, bf16), (SB_W, bf16), (SB_W, f32)]
    return pl.pallas_call(
        _in_proj_kernel,
        grid=(m // bm,),
        in_specs=[row(D_MODEL), per_layer(norm_g), _layer_spec(w_in, layer, single_buffer=True),
                  full(hsum), per_layer(q_g), per_layer(k_g), per_layer(b_f)],
        out_specs=[row(w) for w, _ in outs] + [pl.BlockSpec((FOX_HEADS, bm), lambda i: (0, i))],
        out_shape=([jax.ShapeDtypeStruct((m, w), dt) for w, dt in outs]
                   + [jax.ShapeDtypeStruct((FOX_HEADS, m), f32)]),
        scratch_shapes=[pltpu.VMEM((D_MODEL, D_IN_PAD), jnp.bfloat16)],
        compiler_params=pltpu.CompilerParams(
            dimension_semantics=("arbitrary",), vmem_limit_bytes=VMEM_LIMIT),
    )(x2d, norm_g, w_in, hsum, q_g, k_g, b_f)


def _scan_kernel(lf_ref, tri_ref, prev_ref, c_ref):
    lf = lf_ref[...].reshape(c_ref.shape[1:])
    tri = tri_ref[...]
    within = sum(_dot(p, tri) for p in _bf16_split3(lf))
    total = jnp.broadcast_to(within[:, SCAN_CHUNK - 1:SCAN_CHUNK], within.shape)
    prev = prev_ref[...]
    offset = sum(_dot(prev, p) for p in _bf16_split3(total))
    c_ref[0] = within + offset


def _forget_scan(lf_t, b, tri, prev):
    heads, rows, c = lf_t.shape
    nc = rows // b
    r = heads * nc
    full = lambda a: pl.BlockSpec(a.shape, lambda i: (0,) * a.ndim)
    blk = pl.BlockSpec((1, r, c), lambda i: (i, 0, 0))
    return pl.pallas_call(
        _scan_kernel,
        grid=(b,),
        in_specs=[pl.BlockSpec((heads, nc, c), lambda i: (0, i, 0)), full(tri), full(prev)],
        out_specs=blk,
        out_shape=jax.ShapeDtypeStruct((b, r, c), jnp.float32),
        compiler_params=pltpu.CompilerParams(dimension_semantics=("arbitrary",)),
    )(lf_t, tri, prev)


def _lane_is_first_head(shape):
    return lax.broadcasted_iota(jnp.int32, shape, len(shape) - 1) < HEAD_DIM


def _split_heads(q):
    first = _lane_is_first_head(q.shape)
    zero = jnp.zeros_like(q)
    return jnp.where(first, q, zero), jnp.where(first, zero, q)


def _widen(col, width):
    return jnp.concatenate([col] * (width // LANES), axis=1)


def _fox_kernel(bound_ref, *refs):
    bounded = bound_ref[0, 0] <= FOX_MAX_BOUND

    @pl.when(bounded)
    def _():
        _fox_path(bound_ref, *refs, running_max=False)

    @pl.when(jnp.logical_not(bounded))
    def _():
        _fox_path(bound_ref, *refs, running_max=True)


def _fox_path(bound_ref, q_ref, k_ref, v_ref, ccol_ref, crow_ref, g_ref, o_ref,
              qs_ref, m_ref, acc_ref, ct_ref, *, running_max):
    bq, bk = ATT_BQ, ATT_BK
    qi = pl.program_id(2)
    q0, q1 = _split_heads(q_ref[0])
    qs_ref[0] = q0
    qs_ref[1] = q1
    shift = 0.0 if running_max else bound_ref[0, 0]
    for h in range(2):
        m_ref[h] = jnp.full((bq, LANES), NEG, jnp.float32)
        acc_ref[h] = jnp.zeros((bq, PAIR_W), jnp.float32)
        ct_ref[h] = jnp.broadcast_to(ccol_ref[0, 0, :, h:h + 1], (bq, LANES)) - shift

    def run(blocks, masked, heads=(0, 1)):
        chains = []
        for j, bk, r0 in blocks:
            n = bq - r0
            start = pl.multiple_of(j * bk, bk)
            k = k_ref[0, pl.ds(start, bk), :]
            v = v_ref[0, pl.ds(start, bk), :]
            first = _lane_is_first_head(v.shape)
            ones = jnp.ones_like(v)
            vs = (jnp.where(first, v, ones), jnp.where(first, ones, v))
            keep = None
            if masked:
                keep = (lax.broadcasted_iota(jnp.int32, (n, bk), 1)
                        <= lax.broadcasted_iota(jnp.int32, (n, bk), 0))
            for h in heads:
                cs = crow_ref[0, 0, h:h + 1, pl.ds(start, bk)]
                chains.append((h, slice(r0, bq), bk, k, vs[h], cs, keep))
        qk = [_dot_nt(qs_ref[h, rows, :], k) for h, rows, _, k, _, _, _ in chains]
        ps = []
        for (h, rows, bk, _, _, cs, keep), s in zip(chains, qk):
            s = s + (_widen(ct_ref[h, rows, :], bk) - cs)
            if masked:
                s = jnp.where(keep, s, NEG)
            if running_max:
                m_prev = m_ref[h, rows, :]
                m_next = jnp.maximum(m_prev, jnp.max(s, axis=1, keepdims=True))
                s = s - _widen(m_next, bk)
                acc_ref[h, rows, :] = jnp.exp(m_prev - m_next) * acc_ref[h, rows, :]
                m_ref[h, rows, :] = m_next
            ps.append(jnp.exp(s).astype(jnp.bfloat16))
        for (h, rows, _, _, v, _, _), p in zip(chains, ps):
            acc_ref[h, rows, :] = acc_ref[h, rows, :] + _dot(p, v)

    per_trip = 1 if running_max else FOX_BLOCKS_PER_TRIP
    assert (bq // FOX_BK_FULL) % per_trip == 0
    trips = qi * (bq // FOX_BK_FULL // per_trip)

    def trip(heads):
        def body(j, carry):
            run([(j * per_trip + u, FOX_BK_FULL, 0) for u in range(per_trip)], False, heads)
            return carry
        return body

    if running_max:
        lax.fori_loop(0, trips, trip((0, 1)), 0)
    else:
        span = FOX_BK_FULL * per_trip
        pos = lax.broadcasted_iota(jnp.int32, (1, crow_ref.shape[-1]), 1)
        first_trip = []
        for h in range(2):
            bias = ccol_ref[0, 0, 0:1, h:h + 1] - crow_ref[0, 0, h:h + 1, :]
            dead = jnp.logical_and(bias <= -FOX_CUTOFF, pos < qi * bq)
            n_dead = jnp.sum(jnp.where(dead, 1.0, 0.0)).astype(jnp.int32)
            first_trip.append(n_dead // span)
        both = jnp.maximum(first_trip[0], first_trip[1])
        lax.fori_loop(first_trip[0], both, trip((0,)), 0)
        lax.fori_loop(first_trip[1], both, trip((1,)), 0)
        lax.fori_loop(both, trips, trip((0, 1)), 0)
    sub = bq // bk
    diagonal = [(qi * sub + jj, bk, jj * bk) for jj in range(sub)]
    if running_max:
        for blk in diagonal:
            run([blk], True)
    else:
        run(diagonal, True)

    a0, a1 = acc_ref[0], acc_ref[1]
    o0 = a0 / pltpu.roll(a0, HEAD_DIM, axis=1)
    o1 = a1 / pltpu.roll(a1, HEAD_DIM, axis=1)
    o = jnp.where(_lane_is_first_head(o0.shape), o0, o1)
    o_ref[0] = (o * _silu(g_ref[0])).astype(o_ref.dtype)


def _fox_attention(bound, q, k, v, ccol, crow, gate):
    b, s, w = q.shape
    t = ATT_BQ
    pairs = w // PAIR_W
    qblk = pl.BlockSpec((1, t, PAIR_W), lambda bi, hp, qi: (bi, qi, hp))
    seq = pl.BlockSpec((1, s, PAIR_W), lambda bi, hp, qi: (bi, 0, hp))
    return pl.pallas_call(
        _fox_kernel,
        grid=(b, pairs, s // t),
        in_specs=[pl.BlockSpec(memory_space=pltpu.SMEM), qblk, seq, seq,
                  pl.BlockSpec((1, 1, t, 2), lambda bi, hp, qi: (bi, hp, qi, 0)),
                  pl.BlockSpec((1, 1, 2, s), lambda bi, hp, qi: (bi, hp, 0, 0)),
                  qblk],
        out_specs=qblk,
        out_shape=jax.ShapeDtypeStruct((b, s, w), jnp.bfloat16),
        scratch_shapes=[pltpu.VMEM((2, t, PAIR_W), jnp.bfloat16),
                        pltpu.VMEM((2, t, LANES), jnp.float32),
                        pltpu.VMEM((2, t, PAIR_W), jnp.float32),
                        pltpu.VMEM((2, t, LANES), jnp.float32)],
        compiler_params=pltpu.CompilerParams(
            dimension_semantics=("arbitrary", "arbitrary", "arbitrary"),
            vmem_limit_bytes=VMEM_LIMIT),
    )(bound, q, k, v, ccol, crow, gate)


def _fox_logit_bound(q_gain, k_gain):
    gq = jnp.max(jnp.abs(q_gain), axis=-1)
    gk = jnp.max(jnp.abs(k_gain), axis=-1)
    return (HEAD_DIM * SCALE * 1.01) * gq * gk + 0.05


def _sb_kernel(q_ref, k_ref, v_ref, g_ref, tri_ref, o_ref, qs_ref, r_ref, acc_ref):
    bq, t = ATT_BQ, ATT_BK
    sub = bq // t
    qi = pl.program_id(2)
    q0, q1 = _split_heads(q_ref[0])
    qs_ref[0] = q0
    qs_ref[1] = q1
    for h in range(2):
        r_ref[h] = jnp.zeros((bq, LANES), jnp.float32)
        acc_ref[h] = jnp.zeros((bq, PAIR_W), jnp.float32)

    def step(d, diagonal):
        tri = tri_ref[...]
        if diagonal:
            keep = (lax.broadcasted_iota(jnp.int32, (t, t), 1)
                    < lax.broadcasted_iota(jnp.int32, (t, t), 0))
        chains = [(a, h) for a in range(sub) for h in range(2)]
        rows = {a: slice(a * t, (a + 1) * t) for a in range(sub)}
        valid, ks, vs = {}, {}, {}
        for a in range(sub):
            j = qi * sub + a - d
            valid[a] = j >= 0
            start = pl.multiple_of(jnp.maximum(j, 0) * t, t)
            ks[a] = k_ref[0, pl.ds(start, t), :]
            vs[a] = v_ref[0, pl.ds(start, t), :]
        z = {(a, h): _dot_nt(qs_ref[h, rows[a], :], ks[a]) for a, h in chains}
        parts = {}
        for c in chains:
            sp = _softplus(z[c])
            if diagonal:
                sp = jnp.where(keep, sp, 0.0)
            parts[c] = _bf16_split2(sp)
        cum = {c: _dot(parts[c][0], tri) + _dot(parts[c][1], tri) for c in chains}
        w = {}
        for c in chains:
            wc = jnp.exp(z[c] - cum[c])
            if diagonal:
                wc = jnp.where(keep, wc, 0.0)
            w[c] = wc.astype(jnp.bfloat16)
        for a, h in chains:
            r_prev = r_ref[h, rows[a], :]
            pv = jnp.exp(-r_prev) * _dot(w[a, h], vs[a])
            mass = jnp.broadcast_to(cum[a, h][:, 0:1], (t, LANES))
            if not diagonal:
                pv = jnp.where(valid[a], pv, 0.0)
                mass = jnp.where(valid[a], mass, 0.0)
            acc_ref[h, rows[a], :] = acc_ref[h, rows[a], :] + pv
            r_ref[h, rows[a], :] = r_prev + mass

    def finished(d):
        done = jnp.bool_(True)
        for a in range(sub):
            rows = slice(a * t, (a + 1) * t)
            exhausted = qi * sub + a - (d + 1) < 0
            saturated = jnp.min(r_ref[:, rows, :]) >= SB_CUTOFF
            done = jnp.logical_and(done, jnp.logical_or(exhausted, saturated))
        return done

    step(0, True)

    def cond(carry):
        _, done = carry
        return jnp.logical_not(done)

    def body(carry):
        d, _ = carry
        step(d, False)
        return d + 1, finished(d)

    lax.while_loop(cond, body, (jnp.int32(1), finished(0)))

    o = jnp.where(_lane_is_first_head((bq, PAIR_W)), acc_ref[0], acc_ref[1])
    o_ref[0] = (o * _silu(g_ref[0])).astype(o_ref.dtype)


def _sb_attention(q, k, v, gate, tri):
    b, s, w = q.shape
    t = ATT_BQ
    pairs = w // PAIR_W
    qblk = pl.BlockSpec((1, t, PAIR_W), lambda bi, hp, qi: (bi, qi, hp))
    seq = pl.BlockSpec((1, s, PAIR_W), lambda bi, hp, qi: (bi, 0, hp))
    return pl.pallas_call(
        _sb_kernel,
        grid=(b, pairs, s // t),
        in_specs=[qblk, seq, seq, qblk,
                  pl.BlockSpec(tri.shape, lambda bi, hp, qi: (0, 0))],
        out_specs=qblk,
        out_shape=jax.ShapeDtypeStruct((b, s, w), jnp.bfloat16),
        scratch_shapes=[pltpu.VMEM((2, t, PAIR_W), jnp.bfloat16),
                        pltpu.VMEM((2, t, LANES), jnp.float32),
                        pltpu.VMEM((2, t, PAIR_W), jnp.float32)],
        compiler_params=pltpu.CompilerParams(
            dimension_semantics=("arbitrary", "arbitrary", "arbitrary"),
            vmem_limit_bytes=VMEM_LIMIT),
    )(q, k, v, gate, tri)


def _out_proj_kernel(x_ref, fox_ref, sb_ref, px_ref, hist_ref, pg_ref,
                     wpool_ref, pscale_ref, wout_ref, o_ref, xp_ref, wo_ref, *, blocks_per_seq):
    bm = ROW_BLOCK
    i = pl.program_id(0)

    @pl.when(i == 0)
    def _():
        chunk = D_MIX // WEIGHT_PACK_CHUNKS
        for r in range(0, D_MIX, chunk):
            wo_ref[r:r + chunk, :] = wout_ref[r:r + chunk, :].astype(jnp.bfloat16)

    pos0 = (i % blocks_per_seq) * bm
    hist = hist_ref[...]
    xp_ref[0:MAX_WINDOW, :] = jnp.where(pos0 == 0, jnp.zeros_like(hist), hist)
    px = px_ref[...]
    xp_ref[MAX_WINDOW:MAX_WINDOW + bm, :] = px

    sums = {}
    run = px
    for d in range(1, MAX_WINDOW):
        run = run + xp_ref[MAX_WINDOW - d:MAX_WINDOW - d + bm, :]
        if d + 1 in POOL_WINDOWS:
            sums[d + 1] = run
    group = lax.broadcasted_iota(jnp.int32, (bm, POOL_W), 1) // POOL_GROUP_DIM
    pos = pos0 + lax.broadcasted_iota(jnp.int32, (bm, POOL_W), 0)
    wsum = sums[POOL_WINDOWS[-1]]
    win = jnp.full((bm, POOL_W), POOL_WINDOWS[-1], jnp.int32)
    for g in range(POOL_GROUPS - 2, -1, -1):
        wsum = jnp.where(group == g, sums[POOL_WINDOWS[g]], wsum)
        win = jnp.where(group == g, POOL_WINDOWS[g], win)
    count = jnp.minimum(pos + 1, win).astype(jnp.float32)
    pooled = wsum / count - px
    y = _dot(pooled.astype(jnp.bfloat16), wpool_ref[...]) * pscale_ref[...]
    pool_out = (y * _silu(pg_ref[...])).astype(jnp.bfloat16)

    o_ref[...] = (x_ref[...]
                  + _dot(fox_ref[...], wo_ref[0:FOX_W, :])
                  + _dot(pool_out, wo_ref[FOX_W:FOX_W + POOL_W, :])
                  + _dot(sb_ref[...], wo_ref[FOX_W + POOL_W:D_MIX, :]))


def _out_proj(layer, x2d, fox_o, sb_o, px, pg, wpool_bd, pscale, w_out, seq_len):
    m = x2d.shape[0]
    bm = ROW_BLOCK
    row = lambda w: pl.BlockSpec((bm, w), lambda i: (i, 0))
    per_layer = lambda a: _layer_spec(a, layer)
    hist_per_block = bm // MAX_WINDOW
    hist = pl.BlockSpec((MAX_WINDOW, POOL_W),
                        lambda i: (jnp.maximum(i * hist_per_block - 1, 0), 0))
    kern = functools.partial(_out_proj_kernel, blocks_per_seq=seq_len // bm)
    return pl.pallas_call(
        kern,
        grid=(m // bm,),
        in_specs=[row(D_MODEL), row(FOX_W), row(SB_W), row(POOL_W), hist, row(POOL_W),
                  per_layer(wpool_bd), per_layer(pscale),
                  _layer_spec(w_out, layer, single_buffer=True)],
        out_specs=row(D_MODEL),
        out_shape=jax.ShapeDtypeStruct((m, D_MODEL), jnp.float32),
        scratch_shapes=[pltpu.VMEM((MAX_WINDOW + bm, POOL_W), jnp.float32),
                        pltpu.VMEM((D_MIX, D_MODEL), jnp.bfloat16)],
        compiler_params=pltpu.CompilerParams(
            dimension_semantics=("arbitrary",), vmem_limit_bytes=VMEM_LIMIT),
    )(x2d, fox_o, sb_o, px, px, pg, wpool_bd, pscale, w_out)


def _constants(seq_len):
    bf16 = jnp.bfloat16
    idx = jnp.arange(MXU_TILE)
    hsum =(idx[:, None] // HEAD_DIM == idx[None, :] // HEAD_DIM).astype(bf16)
    c = jnp.arange(SCAN_CHUNK)
    scan_tri = (c[:, None] <= c[None, :]).astype(bf16)
    nc = seq_len // SCAN_CHUNK
    r = jnp.arange(FOX_HEADS * nc)
    scan_prev = ((r[:, None] // nc == r[None, :] // nc)
                 & (r[None, :] < r[:, None])).astype(bf16)
    a = jnp.arange(ATT_BK)
    sb_tri = (a[:, None] >= a[None, :]).astype(bf16)
    return hsum, scan_tri, scan_prev, sb_tri


def _block_diag(w_pool):
    out = jnp.zeros((w_pool.shape[0], POOL_W, POOL_W), w_pool.dtype)
    for g in range(POOL_GROUPS):
        lo = g * POOL_GROUP_DIM
        out = out.at[:, lo:lo + POOL_GROUP_DIM, lo:lo + POOL_GROUP_DIM].set(w_pool[:, g])
    return out.astype(jnp.bfloat16)


def kernel(x, norm_g, w_in, b_f, q_norm_g, k_norm_g, w_pool, pool_scale, w_out):
    b, s, d = x.shape
    depth = norm_g.shape[0]
    assert d == D_MODEL and s % ROW_BLOCK == 0 and s % ATT_BQ == 0 and ATT_BQ % ATT_BK == 0
    assert w_in.shape[-1] == C_FF + FOX_HEADS
    hsum, scan_tri, scan_prev, sb_tri = _constants(s)
    nc = s // SCAN_CHUNK
    x2d = x.reshape(b * s, d)
    wpool_bd = _block_diag(w_pool)
    rowvec = lambda a: a[:, None, :]
    norm_g3, pscale3 = rowvec(norm_g), rowvec(pool_scale)
    b_f3 = rowvec(jnp.pad(b_f, ((0, 0), (0, FF_PAD - FOX_HEADS))))
    q_g = rowvec(jnp.tile(q_norm_g, (1, FOX_HEADS)))
    k_g = rowvec(jnp.tile(k_norm_g, (1, FOX_HEADS)))
    bounds = _fox_logit_bound(q_norm_g, k_norm_g)[:, None]
    for l in range(depth):
        (fq, fk, fv, fg, px, pg, sq, sk, sv, sg, lf) = _in_proj(
            l, x2d, norm_g3, w_in, hsum, q_g, k_g, b_f3)

        c = _forget_scan(lf.reshape(FOX_HEADS, b * nc, SCAN_CHUNK), b, scan_tri, scan_prev)
        crow = c.reshape(b, FOX_HEADS // 2, 2, s)
        ccol = crow.transpose(0, 1, 3, 2)

        r3 = lambda a: a.reshape(b, s, a.shape[-1])
        fox_o = _fox_attention(bounds[l:l + 1], r3(fq), r3(fk), r3(fv), ccol, crow, r3(fg))
        sb_o = _sb_attention(r3(sq), r3(sk), r3(sv), r3(sg), sb_tri)

        x2d = _out_proj(l, x2d, fox_o.reshape(b * s, FOX_W), sb_o.reshape(b * s, SB_W),
                        px, pg, wpool_bd, pscale3, w_out, s)
    return x2d.reshape(b, s, d)
```

```python
import functools

import jax
import jax.numpy as jnp
from jax import lax
from jax.experimental import pallas as pl
from jax.experimental.pallas import tpu as pltpu

D_MODEL = 1024
HEAD_DIM = 64
FOX_HEADS = 8
SB_HEADS = 4
POOL_GROUPS = 4
POOL_WINDOWS = (2, 4, 8, 16)
POOL_GROUP_DIM = 64
FOX_W = FOX_HEADS * HEAD_DIM
SB_W = SB_HEADS * HEAD_DIM
POOL_W = POOL_GROUPS * POOL_GROUP_DIM
D_MIX = FOX_W + POOL_W + SB_W
EPS = 1e-6
NEG = -1e30
SCALE = HEAD_DIM ** -0.5
LOG2E = 1.4426950408889634
FOX_MAX_BOUND = 30.0
SB_CUTOFF = 106.0
FOX_CUTOFF = 106.0

LANES = 128
MXU_TILE = 256
PAIR_W = 2 * HEAD_DIM
MAX_WINDOW = max(POOL_WINDOWS)
FF_PAD = LANES

ROW_BLOCK = 1024
WEIGHT_PACK_CHUNKS = 8
ATT_BQ = 1024
ATT_BK = 256
FOX_BK_FULL = 512
FOX_BLOCKS_PER_TRIP = 2
SCAN_CHUNK = LANES

VMEM_LIMIT = 56 * 1024 * 1024

C_FQ, C_FK, C_FV, C_FG = 0, FOX_W, 2 * FOX_W, 3 * FOX_W
C_PX = 4 * FOX_W
C_PG = C_PX + POOL_W
C_SQ = C_PG + POOL_W
C_SK = C_SQ + SB_W
C_SV = C_SK + SB_W
C_SG = C_SV + SB_W
C_FF = C_SG + SB_W
D_IN_PAD = C_FF + FF_PAD


def _bf16_split2(x):
    hi = x.astype(jnp.bfloat16)
    lo = (x - hi.astype(jnp.float32)).astype(jnp.bfloat16)
    return hi, lo


def _bf16_split3(x):
    a = x.astype(jnp.bfloat16)
    r = x - a.astype(jnp.float32)
    b = r.astype(jnp.bfloat16)
    c = (r - b.astype(jnp.float32)).astype(jnp.bfloat16)
    return a, b, c


def _dot(a, b):
    return jnp.dot(a, b, preferred_element_type=jnp.float32)


def _dot_nt(a, b):
    return lax.dot_general(a, b, (((1,), (1,)), ((), ())),
                           preferred_element_type=jnp.float32)


def _silu(x):
    return x * (1.0 / (1.0 + jnp.exp(-x)))


def _softplus(x):
    return jnp.maximum(x, 0.0) + jnp.log(1.0 + jnp.exp2(jnp.abs(x) * (-LOG2E)))


def _in_proj_kernel(x_ref, g_ref, win_ref, hsum_ref, qg_ref, kg_ref, bf_ref,
                    fq_ref, fk_ref, fv_ref, fg_ref, px_ref, pg_ref,
                    sq_ref, sk_ref, sv_ref, sg_ref, lf_ref, w_ref):
    @pl.when(pl.program_id(0) == 0)
    def _():
        chunk = D_MODEL // WEIGHT_PACK_CHUNKS
        for r in range(0, D_MODEL, chunk):
            rows = slice(r, r + chunk)
            w_ref[rows, 0:C_PX] = win_ref[rows, 0:C_PX].astype(jnp.bfloat16)
            w_ref[rows, C_PX:C_FF] = win_ref[rows, C_PX + FOX_HEADS:].astype(jnp.bfloat16)
            ff = win_ref[rows, C_PX:C_PX + FOX_HEADS].astype(jnp.bfloat16)
            w_ref[rows, C_FF:] = jnp.concatenate(
                [ff, jnp.zeros((chunk, FF_PAD - FOX_HEADS), jnp.bfloat16)], axis=1)

    x = x_ref[...]
    ms = jnp.mean(x * x, axis=-1, keepdims=True)
    h = (x * lax.rsqrt(ms + EPS) * g_ref[...]).astype(jnp.bfloat16)

    def proj(c0, width):
        return _dot(h, w_ref[:, c0:c0 + width])

    def head_norm(y, gain):
        hi, lo = _bf16_split2(y * y)
        hsum = hsum_ref[...]
        w = hsum.shape[0]
        ssq = jnp.concatenate(
            [_dot(hi[:, c:c + w], hsum) + _dot(lo[:, c:c + w], hsum)
             for c in range(0, y.shape[1], w)], axis=1)
        return y * lax.rsqrt(ssq * (1.0 / HEAD_DIM) + EPS) * gain

    fq_ref[...] = (head_norm(proj(C_FQ, FOX_W), qg_ref[...]) * SCALE).astype(jnp.bfloat16)
    fk_ref[...] = head_norm(proj(C_FK, FOX_W), kg_ref[...]).astype(jnp.bfloat16)
    fv_ref[...] = proj(C_FV, FOX_W).astype(jnp.bfloat16)
    fg_ref[...] = proj(C_FG, FOX_W)
    px_ref[...] = proj(C_PX, POOL_W)
    pg_ref[...] = proj(C_PG, POOL_W)
    sq_ref[...] = (proj(C_SQ, SB_W) * SCALE).astype(jnp.bfloat16)
    sk_ref[...] = proj(C_SK, SB_W).astype(jnp.bfloat16)
    sv_ref[...] = proj(C_SV, SB_W).astype(jnp.bfloat16)
    sg_ref[...] = proj(C_SG, SB_W)
    ff = proj(C_FF, FF_PAD) + bf_ref[...]
    lf_ref[...] = (-_softplus(-ff)).T[:FOX_HEADS, :]


def _layer_spec(stacked, layer, single_buffer=False):
    tail = stacked.shape[1:]
    mode = dict(pipeline_mode=pl.Buffered(1)) if single_buffer else {}
    return pl.BlockSpec((None,) + tail, lambda i: (layer,) + (0,) * len(tail), **mode)


def _in_proj(layer, x2d, norm_g, w_in, hsum, q_g, k_g, b_f):
    m = x2d.shape[0]
    bm = ROW_BLOCK
    row = lambda w: pl.BlockSpec((bm, w), lambda i: (i, 0))
    full = lambda a: pl.BlockSpec(a.shape, lambda i: (0,) * a.ndim)
    per_layer = lambda a: _layer_spec(a, layer)
    bf16, f32 = jnp.bfloat16, jnp.float32
    outs = [(FOX_W, bf16), (FOX_W, bf16), (FOX_W, bf16), (FOX_W, f32),
            (POOL_W, f32), (POOL_W, f32),
            (SB_W, bf16), (SB_W, bf16), (SB_W, bf16), (SB_W, f32)]
    return pl.pallas_call(
        _in_proj_kernel,
        grid=(m // bm,),
        in_specs=[row(D_MODEL), per_layer(norm_g), _layer_spec(w_in, layer, single_buffer=True),
                  full(hsum), per_layer(q_g), per_layer(k_g), per_layer(b_f)],
        out_specs=[row(w) for w, _ in outs] + [pl.BlockSpec((FOX_HEADS, bm), lambda i: (0, i))],
        out_shape=([jax.ShapeDtypeStruct((m, w), dt) for w, dt in outs]
                   + [jax.ShapeDtypeStruct((FOX_HEADS, m), f32)]),
        scratch_shapes=[pltpu.VMEM((D_MODEL, D_IN_PAD), jnp.bfloat16)],
        compiler_params=pltpu.CompilerParams(
            dimension_semantics=("arbitrary",), vmem_limit_bytes=VMEM_LIMIT),
    )(x2d, norm_g, w_in, hsum, q_g, k_g, b_f)


def _scan_kernel(lf_ref, tri_ref, prev_ref, c_ref):
    lf = lf_ref[...].reshape(c_ref.shape[1:])
    tri = tri_ref[...]
    within = sum(_dot(p, tri) for p in _bf16_split3(lf))
    total = jnp.broadcast_to(within[:, SCAN_CHUNK - 1:SCAN_CHUNK], within.shape)
    prev = prev_ref[...]
    offset = sum(_dot(prev, p) for p in _bf16_split3(total))
    c_ref[0] = within + offset


def _forget_scan(lf_t, b, tri, prev):
    heads, rows, c = lf_t.shape
    nc = rows // b
    r = heads * nc
    full = lambda a: pl.BlockSpec(a.shape, lambda i: (0,) * a.ndim)
    blk = pl.BlockSpec((1, r, c), lambda i: (i, 0, 0))
    return pl.pallas_call(
        _scan_kernel,
        grid=(b,),
        in_specs=[pl.BlockSpec((heads, nc, c), lambda i: (0, i, 0)), full(tri), full(prev)],
        out_specs=blk,
        out_shape=jax.ShapeDtypeStruct((b, r, c), jnp.float32),
        compiler_params=pltpu.CompilerParams(dimension_semantics=("arbitrary",)),
    )(lf_t, tri, prev)


def _lane_is_first_head(shape):
    return lax.broadcasted_iota(jnp.int32, shape, len(shape) - 1) < HEAD_DIM


def _split_heads(q):
    first = _lane_is_first_head(q.shape)
    zero = jnp.zeros_like(q)
    return jnp.where(first, q, zero), jnp.where(first, zero, q)


def _widen(col, width):
    return jnp.concatenate([col] * (width // LANES), axis=1)


def _fox_kernel(bound_ref, *refs):
    bounded = bound_ref[0, 0] <= FOX_MAX_BOUND

    @pl.when(bounded)
    def _():
        _fox_path(bound_ref, *refs, running_max=False)

    @pl.when(jnp.logical_not(bounded))
    def _():
        _fox_path(bound_ref, *refs, running_max=True)


def _fox_path(bound_ref, q_ref, k_ref, v_ref, ccol_ref, crow_ref, g_ref, o_ref,
              qs_ref, m_ref, acc_ref, ct_ref, *, running_max):
    bq, bk = ATT_BQ, ATT_BK
    qi = pl.program_id(2)
    q0, q1 = _split_heads(q_ref[0])
    qs_ref[0] = q0
    qs_ref[1] = q1
    shift = 0.0 if running_max else bound_ref[0, 0]
    c_all = ccol_ref[0]
    head_lane = lax.broadcasted_iota(jnp.int32, c_all.shape, 1)
    c_first = []
    for h in range(2):
        m_ref[h] = jnp.full((bq, LANES), NEG, jnp.float32)
        acc_ref[h] = jnp.zeros((bq, PAIR_W), jnp.float32)
        mine = head_lane == 2 * pl.program_id(1) + h
        ct = jnp.sum(jnp.where(mine, c_all, 0.0), axis=1, keepdims=True)
        c_first.append(ct[0:1, :])
        ct_ref[h] = jnp.broadcast_to(ct, (bq, LANES)) - shift

    def run(blocks, masked, heads=(0, 1)):
        chains = []
        for j, bk, r0 in blocks:
            n = bq - r0
            start = pl.multiple_of(j * bk, bk)
            k = k_ref[0, pl.ds(start, bk), :]
            v = v_ref[0, pl.ds(start, bk), :]
            first = _lane_is_first_head(v.shape)
            ones = jnp.ones_like(v)
            vs = (jnp.where(first, v, ones), jnp.where(first, ones, v))
            keep = None
            if masked:
                keep = (lax.broadcasted_iota(jnp.int32, (n, bk), 1)
                        <= lax.broadcasted_iota(jnp.int32, (n, bk), 0))
            for h in heads:
                cs = crow_ref[0, 0, h:h + 1, pl.ds(start, bk)]
                chains.append((h, slice(r0, bq), bk, k, vs[h], cs, keep))
        qk = [_dot_nt(qs_ref[h, rows, :], k) for h, rows, _, k, _, _, _ in chains]
        ps = []
        for (h, rows, bk, _, _, cs, keep), s in zip(chains, qk):
            s = s + (_widen(ct_ref[h, rows, :], bk) - cs)
            if masked:
                s = jnp.where(keep, s, NEG)
            if running_max:
                m_prev = m_ref[h, rows, :]
                m_next = jnp.maximum(m_prev, jnp.max(s, axis=1, keepdims=True))
                s = s - _widen(m_next, bk)
                acc_ref[h, rows, :] = jnp.exp(m_prev - m_next) * acc_ref[h, rows, :]
                m_ref[h, rows, :] = m_next
            ps.append(jnp.exp(s).astype(jnp.bfloat16))
        for (h, rows, _, _, v, _, _), p in zip(chains, ps):
            acc_ref[h, rows, :] = acc_ref[h, rows, :] + _dot(p, v)

    sub = bq // bk
    diagonal = [(qi * sub + jj, bk, jj * bk) for jj in range(sub)]
    if running_max:
        for blk in diagonal:
            run([blk], True)
    else:
        run(diagonal, True)

    per_trip = 1 if running_max else FOX_BLOCKS_PER_TRIP
    assert (bq // FOX_BK_FULL) % per_trip == 0
    trips = qi * (bq // FOX_BK_FULL // per_trip)

    def trip(heads):
        def body(j, carry):
            run([(j * per_trip + u, FOX_BK_FULL, 0) for u in range(per_trip)], False, heads)
            return carry
        return body

    if running_max:
        lax.fori_loop(0, trips, trip((0, 1)), 0)
    else:
        span = FOX_BK_FULL * per_trip
        pos = lax.broadcasted_iota(jnp.int32, (1, crow_ref.shape[-1]), 1)
        first_trip = []
        for h in range(2):
            bias = c_first[h] - crow_ref[0, 0, h:h + 1, :]
            dead = jnp.logical_and(bias <= -FOX_CUTOFF, pos < qi * bq)
            n_dead = jnp.sum(jnp.where(dead, 1.0, 0.0)).astype(jnp.int32)
            first_trip.append(n_dead // span)
        both = jnp.maximum(first_trip[0], first_trip[1])
        lax.fori_loop(first_trip[0], both, trip((0,)), 0)
        lax.fori_loop(first_trip[1], both, trip((1,)), 0)
        lax.fori_loop(both, trips, trip((0, 1)), 0)

    a0, a1 = acc_ref[0], acc_ref[1]
    o0 = a0 / pltpu.roll(a0, HEAD_DIM, axis=1)
    o1 = a1 / pltpu.roll(a1, HEAD_DIM, axis=1)
    o = jnp.where(_lane_is_first_head(o0.shape), o0, o1)
    o_ref[0] = (o * _silu(g_ref[0])).astype(o_ref.dtype)


def _fox_attention(bound, q, k, v, ccol, crow, gate):
    b, s, w = q.shape
    t = ATT_BQ
    pairs = w // PAIR_W
    qblk = pl.BlockSpec((1, t, PAIR_W), lambda bi, hp, qi: (bi, qi, hp))
    seq = pl.BlockSpec((1, s, PAIR_W), lambda bi, hp, qi: (bi, 0, hp))
    return pl.pallas_call(
        _fox_kernel,
        grid=(b, pairs, s // t),
        in_specs=[pl.BlockSpec(memory_space=pltpu.SMEM), qblk, seq, seq,
                  pl.BlockSpec((1, t, ccol.shape[-1]), lambda bi, hp, qi: (bi, qi, 0)),
                  pl.BlockSpec((1, 1, 2, s), lambda bi, hp, qi: (bi, hp, 0, 0)),
                  qblk],
        out_specs=qblk,
        out_shape=jax.ShapeDtypeStruct((b, s, w), jnp.bfloat16),
        scratch_shapes=[pltpu.VMEM((2, t, PAIR_W), jnp.bfloat16),
                        pltpu.VMEM((2, t, LANES), jnp.float32),
                        pltpu.VMEM((2, t, PAIR_W), jnp.float32),
                        pltpu.VMEM((2, t, LANES), jnp.float32)],
        compiler_params=pltpu.CompilerParams(
            dimension_semantics=("arbitrary", "arbitrary", "arbitrary"),
            vmem_limit_bytes=VMEM_LIMIT),
    )(bound, q, k, v, ccol, crow, gate)


def _fox_logit_bound(q_gain, k_gain):
    gq = jnp.max(jnp.abs(q_gain), axis=-1)
    gk = jnp.max(jnp.abs(k_gain), axis=-1)
    return (HEAD_DIM * SCALE * 1.01) * gq * gk + 0.05


def _sb_kernel(q_ref, k_ref, v_ref, g_ref, tri_ref, o_ref, qs_ref, r_ref, acc_ref):
    bq, t = ATT_BQ, ATT_BK
    sub = bq // t
    qi = pl.program_id(2)
    q0, q1 = _split_heads(q_ref[0])
    qs_ref[0] = q0
    qs_ref[1] = q1
    for h in range(2):
        r_ref[h] = jnp.zeros((bq, LANES), jnp.float32)
        acc_ref[h] = jnp.zeros((bq, PAIR_W), jnp.float32)

    def step(d, diagonal):
        tri = tri_ref[...]
        if diagonal:
            keep = (lax.broadcasted_iota(jnp.int32, (t, t), 1)
                    < lax.broadcasted_iota(jnp.int32, (t, t), 0))
        chains = [(a, h) for a in range(sub) for h in range(2)]
        rows = {a: slice(a * t, (a + 1) * t) for a in range(sub)}
        valid, ks, vs = {}, {}, {}
        for a in range(sub):
            j = qi * sub + a - d
            valid[a] = j >= 0
            start = pl.multiple_of(jnp.maximum(j, 0) * t, t)
            ks[a] = k_ref[0, pl.ds(start, t), :]
            vs[a] = v_ref[0, pl.ds(start, t), :]
        z = {(a, h): _dot_nt(qs_ref[h, rows[a], :], ks[a]) for a, h in chains}
        parts = {}
        for c in chains:
            sp = _softplus(z[c])
            if diagonal:
                sp = jnp.where(keep, sp, 0.0)
            parts[c] = _bf16_split2(sp)
        cum = {c: _dot(parts[c][0], tri) + _dot(parts[c][1], tri) for c in chains}
        w = {}
        for c in chains:
            wc = jnp.exp(z[c] - cum[c])
            if diagonal:
                wc = jnp.where(keep, wc, 0.0)
            w[c] = wc.astype(jnp.bfloat16)
        for a, h in chains:
            r_prev = r_ref[h, rows[a], :]
            pv = jnp.exp(-r_prev) * _dot(w[a, h], vs[a])
            mass = jnp.broadcast_to(cum[a, h][:, 0:1], (t, LANES))
            if not diagonal:
                pv = jnp.where(valid[a], pv, 0.0)
                mass = jnp.where(valid[a], mass, 0.0)
            acc_ref[h, rows[a], :] = acc_ref[h, rows[a], :] + pv
            r_ref[h, rows[a], :] = r_prev + mass

    def finished(d):
        done = jnp.bool_(True)
        for a in range(sub):
            rows = slice(a * t, (a + 1) * t)
            exhausted = qi * sub + a - (d + 1) < 0
            saturated = jnp.min(r_ref[:, rows, :]) >= SB_CUTOFF
            done = jnp.logical_and(done, jnp.logical_or(exhausted, saturated))
        return done

    step(0, True)

    def cond(carry):
        _, done = carry
        return jnp.logical_not(done)

    def body(carry):
        d, _ = carry
        step(d, False)
        return d + 1, finished(d)

    lax.while_loop(cond, body, (jnp.int32(1), finished(0)))

    o = jnp.where(_lane_is_first_head((bq, PAIR_W)), acc_ref[0], acc_ref[1])
    o_ref[0] = (o * _silu(g_ref[0])).astype(o_ref.dtype)


def _sb_attention(q, k, v, gate, tri):
    b, s, w = q.shape
    t = ATT_BQ
    pairs = w // PAIR_W
    qblk = pl.BlockSpec((1, t, PAIR_W), lambda bi, hp, qi: (bi, qi, hp))
    seq = pl.BlockSpec((1, s, PAIR_W), lambda bi, hp, qi: (bi, 0, hp))
    return pl.pallas_call(
        _sb_kernel,
        grid=(b, pairs, s // t),
        in_specs=[qblk, seq, seq, qblk,
                  pl.BlockSpec(tri.shape, lambda bi, hp, qi: (0, 0))],
        out_specs=qblk,
        out_shape=jax.ShapeDtypeStruct((b, s, w), jnp.bfloat16),
        scratch_shapes=[pltpu.VMEM((2, t, PAIR_W), jnp.bfloat16),
                        pltpu.VMEM((2, t, LANES), jnp.float32),
                        pltpu.VMEM((2, t, PAIR_W), jnp.float32)],
        compiler_params=pltpu.CompilerParams(
            dimension_semantics=("arbitrary", "arbitrary", "arbitrary"),
            vmem_limit_bytes=VMEM_LIMIT),
    )(q, k, v, gate, tri)


def _out_proj_kernel(x_ref, fox_ref, sb_ref, px_ref, hist_ref, pg_ref,
                     wpool_ref, pscale_ref, wout_ref, o_ref, xp_ref, wo_ref, *, blocks_per_seq):
    bm = ROW_BLOCK
    i = pl.program_id(0)

    @pl.when(i == 0)
    def _():
        chunk = D_MIX // WEIGHT_PACK_CHUNKS
        for r in range(0, D_MIX, chunk):
            wo_ref[r:r + chunk, :] = wout_ref[r:r + chunk, :].astype(jnp.bfloat16)

    pos0 = (i % blocks_per_seq) * bm
    hist = hist_ref[...]
    xp_ref[0:MAX_WINDOW, :] = jnp.where(pos0 == 0, jnp.zeros_like(hist), hist)
    px = px_ref[...]
    xp_ref[MAX_WINDOW:MAX_WINDOW + bm, :] = px

    sums = {}
    run = px
    for d in range(1, MAX_WINDOW):
        run = run + xp_ref[MAX_WINDOW - d:MAX_WINDOW - d + bm, :]
        if d + 1 in POOL_WINDOWS:
            sums[d + 1] = run
    group = lax.broadcasted_iota(jnp.int32, (bm, POOL_W), 1) // POOL_GROUP_DIM
    pos = pos0 + lax.broadcasted_iota(jnp.int32, (bm, POOL_W), 0)
    wsum = sums[POOL_WINDOWS[-1]]
    win = jnp.full((bm, POOL_W), POOL_WINDOWS[-1], jnp.int32)
    for g in range(POOL_GROUPS - 2, -1, -1):
        wsum = jnp.where(group == g, sums[POOL_WINDOWS[g]], wsum)
        win = jnp.where(group == g, POOL_WINDOWS[g], win)
    count = jnp.minimum(pos + 1, win).astype(jnp.float32)
    pooled = wsum / count - px
    y = _dot(pooled.astype(jnp.bfloat16), wpool_ref[...]) * pscale_ref[...]
    pool_out = (y * _silu(pg_ref[...])).astype(jnp.bfloat16)

    o_ref[...] = (x_ref[...]
                  + _dot(fox_ref[...], wo_ref[0:FOX_W, :])
                  + _dot(pool_out, wo_ref[FOX_W:FOX_W + POOL_W, :])
                  + _dot(sb_ref[...], wo_ref[FOX_W + POOL_W:D_MIX, :]))


def _out_proj(layer, x2d, fox_o, sb_o, px, pg, wpool_bd, pscale, w_out, seq_len):
    m = x2d.shape[0]
    bm = ROW_BLOCK
    row = lambda w: pl.BlockSpec((bm, w), lambda i: (i, 0))
    per_layer = lambda a: _layer_spec(a, layer)
    hist_per_block = bm // MAX_WINDOW
    hist = pl.BlockSpec((MAX_WINDOW, POOL_W),
                        lambda i: (jnp.maximum(i * hist_per_block - 1, 0), 0))
    kern = functools.partial(_out_proj_kernel, blocks_per_seq=seq_len // bm)
    return pl.pallas_call(
        kern,
        grid=(m // bm,),
        in_specs=[row(D_MODEL), row(FOX_W), row(SB_W), row(POOL_W), hist, row(POOL_W),
                  per_layer(wpool_bd), per_layer(pscale),
                  _layer_spec(w_out, layer, single_buffer=True)],
        out_specs=row(D_MODEL),
        out_shape=jax.ShapeDtypeStruct((m, D_MODEL), jnp.float32),
        scratch_shapes=[pltpu.VMEM((MAX_WINDOW + bm, POOL_W), jnp.float32),
                        pltpu.VMEM((D_MIX, D_MODEL), jnp.bfloat16)],
        compiler_params=pltpu.CompilerParams(
            dimension_semantics=("arbitrary",), vmem_limit_bytes=VMEM_LIMIT),
    )(x2d, fox_o, sb_o, px, px, pg, wpool_bd, pscale, w_out)


def _constants(seq_len):
    bf16 = jnp.bfloat16
    idx = jnp.arange(MXU_TILE)
    hsum =(idx[:, None] // HEAD_DIM == idx[None, :] // HEAD_DIM).astype(bf16)
    c = jnp.arange(SCAN_CHUNK)
    scan_tri = (c[:, None] <= c[None, :]).astype(bf16)
    nc = seq_len // SCAN_CHUNK
    r = jnp.arange(FOX_HEADS * nc)
    scan_prev = ((r[:, None] // nc == r[None, :] // nc)
                 & (r[None, :] < r[:, None])).astype(bf16)
    a = jnp.arange(ATT_BK)
    sb_tri = (a[:, None] >= a[None, :]).astype(bf16)
    return hsum, scan_tri, scan_prev, sb_tri


def _block_diag(w_pool):
    out = jnp.zeros((w_pool.shape[0], POOL_W, POOL_W), w_pool.dtype)
    for g in range(POOL_GROUPS):
        lo = g * POOL_GROUP_DIM
        out = out.at[:, lo:lo + POOL_GROUP_DIM, lo:lo + POOL_GROUP_DIM].set(w_pool[:, g])
    return out.astype(jnp.bfloat16)


def kernel(x, norm_g, w_in, b_f, q_norm_g, k_norm_g, w_pool, pool_scale, w_out):
    b, s, d = x.shape
    depth = norm_g.shape[0]
    assert d == D_MODEL and s % ROW_BLOCK == 0 and s % ATT_BQ == 0 and ATT_BQ % ATT_BK == 0
    assert w_in.shape[-1] == C_FF + FOX_HEADS
    hsum, scan_tri, scan_prev, sb_tri = _constants(s)
    nc = s // SCAN_CHUNK
    x2d = x.reshape(b * s, d)
    wpool_bd = _block_diag(w_pool)
    w_in_bf = w_in.astype(jnp.bfloat16)
    rowvec = lambda a: a[:, None, :]
    norm_g3, pscale3 = rowvec(norm_g), rowvec(pool_scale)
    b_f3 = rowvec(jnp.pad(b_f, ((0, 0), (0, FF_PAD - FOX_HEADS))))
    q_g = rowvec(jnp.tile(q_norm_g, (1, FOX_HEADS)))
    k_g = rowvec(jnp.tile(k_norm_g, (1, FOX_HEADS)))
    bounds = _fox_logit_bound(q_norm_g, k_norm_g)[:, None]
    for l in range(depth):
        (fq, fk, fv, fg, px, pg, sq, sk, sv, sg, lf) = _in_proj(
            l, x2d, norm_g3, w_in_bf, hsum, q_g, k_g, b_f3)

        c = _forget_scan(lf.reshape(FOX_HEADS, b * nc, SCAN_CHUNK), b, scan_tri, scan_prev)
        crow = c.reshape(b, FOX_HEADS // 2, 2, s)
        ccol = c.reshape(b, FOX_HEADS, s).transpose(0, 2, 1)

        r3 = lambda a: a.reshape(b, s, a.shape[-1])
        fox_o = _fox_attention(bounds[l:l + 1], r3(fq), r3(fk), r3(fv), ccol, crow, r3(fg))
        sb_o = _sb_attention(r3(sq), r3(sk), r3(sv), r3(sg), sb_tri)

        x2d = _out_proj(l, x2d, fox_o.reshape(b * s, FOX_W), sb_o.reshape(b * s, SB_W),
                        px, pg, wpool_bd, pscale3, w_out, s)
    return x2d.reshape(b, s, d)
```

```python
import functools

import jax
import jax.numpy as jnp
from jax import lax
from jax.experimental import pallas as pl
from jax.experimental.pallas import tpu as pltpu

D_MODEL = 1024
HEAD_DIM = 64
FOX_HEADS = 8
SB_HEADS = 4
POOL_GROUPS = 4
POOL_WINDOWS = (2, 4, 8, 16)
POOL_GROUP_DIM = 64
FOX_W = FOX_HEADS * HEAD_DIM
SB_W = SB_HEADS * HEAD_DIM
POOL_W = POOL_GROUPS * POOL_GROUP_DIM
D_MIX = FOX_W + POOL_W + SB_W
EPS = 1e-6
NEG = -1e30
SCALE = HEAD_DIM ** -0.5
LOG2E = 1.4426950408889634
FOX_MAX_BOUND = 30.0
SB_CUTOFF = 106.0
FOX_CUTOFF = 106.0

LANES = 128
MXU_TILE = 256
PAIR_W = 2 * HEAD_DIM
MAX_WINDOW = max(POOL_WINDOWS)
FF_PAD = LANES

ROW_BLOCK = 1024
WEIGHT_PACK_CHUNKS = 8
ATT_BQ = 1024
ATT_BK = 256
FOX_BK_FULL = 512
FOX_BLOCKS_PER_TRIP = 2
SCAN_CHUNK = LANES

VMEM_LIMIT = 56 * 1024 * 1024

C_FQ, C_FK, C_FV, C_FG = 0, FOX_W, 2 * FOX_W, 3 * FOX_W
C_PX = 4 * FOX_W
C_PG = C_PX + POOL_W
C_SQ = C_PG + POOL_W
C_SK = C_SQ + SB_W
C_SV = C_SK + SB_W
C_SG = C_SV + SB_W
C_FF = C_SG + SB_W
D_IN_PAD = C_FF + FF_PAD


def _bf16_split3(x):
    a = x.astype(jnp.bfloat16)
    r = x - a.astype(jnp.float32)
    b = r.astype(jnp.bfloat16)
    c = (r - b.astype(jnp.float32)).astype(jnp.bfloat16)
    return a, b, c


def _dot(a, b):
    return jnp.dot(a, b, preferred_element_type=jnp.float32)


def _dot_nt(a, b):
    return lax.dot_general(a, b, (((1,), (1,)), ((), ())),
                           preferred_element_type=jnp.float32)


def _silu(x):
    return x * (1.0 / (1.0 + jnp.exp(-x)))


def _softplus(x):
    return jnp.maximum(x, 0.0) + jnp.log(1.0 + jnp.exp2(jnp.abs(x) * (-LOG2E)))


def _in_proj_kernel(x_ref, g_ref, win_ref, hsum_ref, qg_ref, kg_ref, bf_ref,
                    fq_ref, fk_ref, fv_ref, fg_ref, px_ref, pg_ref,
                    sq_ref, sk_ref, sv_ref, sg_ref, lf_ref, w_ref):
    @pl.when(pl.program_id(0) == 0)
    def _():
        chunk = D_MODEL // WEIGHT_PACK_CHUNKS
        for r in range(0, D_MODEL, chunk):
            rows = slice(r, r + chunk)
            w_ref[rows, 0:C_PX] = win_ref[rows, 0:C_PX].astype(jnp.bfloat16)
            w_ref[rows, C_PX:C_FF] = win_ref[rows, C_PX + FOX_HEADS:].astype(jnp.bfloat16)
            ff = win_ref[rows, C_PX:C_PX + FOX_HEADS].astype(jnp.bfloat16)
            w_ref[rows, C_FF:] = jnp.concatenate(
                [ff, jnp.zeros((chunk, FF_PAD - FOX_HEADS), jnp.bfloat16)], axis=1)

    x = x_ref[...]
    ms = jnp.mean(x * x, axis=-1, keepdims=True)
    h = (x * lax.rsqrt(ms + EPS) * g_ref[...]).astype(jnp.bfloat16)

    def proj(c0, width):
        return _dot(h, w_ref[:, c0:c0 + width])

    def head_norm(y, gain):
        sq = (y * y).astype(jnp.bfloat16)
        hsum = hsum_ref[...]
        w = hsum.shape[0]
        ssq = jnp.concatenate(
            [_dot(sq[:, c:c + w], hsum) for c in range(0, y.shape[1], w)], axis=1)
        return y * lax.rsqrt(ssq * (1.0 / HEAD_DIM) + EPS) * gain

    fq_ref[...] = (head_norm(proj(C_FQ, FOX_W), qg_ref[...]) * SCALE).astype(jnp.bfloat16)
    fk_ref[...] = head_norm(proj(C_FK, FOX_W), kg_ref[...]).astype(jnp.bfloat16)
    fv_ref[...] = proj(C_FV, FOX_W).astype(jnp.bfloat16)
    fg_ref[...] = proj(C_FG, FOX_W)
    px_ref[...] = proj(C_PX, POOL_W)
    pg_ref[...] = proj(C_PG, POOL_W)
    sq_ref[...] = (proj(C_SQ, SB_W) * SCALE).astype(jnp.bfloat16)
    sk_ref[...] = proj(C_SK, SB_W).astype(jnp.bfloat16)
    sv_ref[...] = proj(C_SV, SB_W).astype(jnp.bfloat16)
    sg_ref[...] = proj(C_SG, SB_W)
    ff = proj(C_FF, FF_PAD) + bf_ref[...]
    lf_ref[...] = (-_softplus(-ff)).T[:FOX_HEADS, :]


def _layer_spec(stacked, layer, single_buffer=False):
    tail = stacked.shape[1:]
    mode = dict(pipeline_mode=pl.Buffered(1)) if single_buffer else {}
    return pl.BlockSpec((None,) + tail, lambda i: (layer,) + (0,) * len(tail), **mode)


def _in_proj(layer, x2d, norm_g, w_in, hsum, q_g, k_g, b_f):
    m = x2d.shape[0]
    bm = ROW_BLOCK
    row = lambda w: pl.BlockSpec((bm, w), lambda i: (i, 0))
    full = lambda a: pl.BlockSpec(a.shape, lambda i: (0,) * a.ndim)
    per_layer = lambda a: _layer_spec(a, layer)
    bf16, f32 = jnp.bfloat16, jnp.float32
    outs = [(FOX_W, bf16), (FOX_W, bf16), (FOX_W, bf16), (FOX_W, f32),
            (POOL_W, f32), (POOL_W, f32),
            (SB_W, bf16), (SB_W, bf16), (SB_W, bf16), (SB_W, f32)]
    return pl.pallas_call(
        _in_proj_kernel,
        grid=(m // bm,),
        in_specs=[row(D_MODEL), per_layer(norm_g), _layer_spec(w_in, layer, single_buffer=True),
                  full(hsum), per_layer(q_g), per_layer(k_g), per_layer(b_f)],
        out_specs=[row(w) for w, _ in outs] + [pl.BlockSpec((FOX_HEADS, bm), lambda i: (0, i))],
        out_shape=([jax.ShapeDtypeStruct((m, w), dt) for w, dt in outs]
                   + [jax.ShapeDtypeStruct((FOX_HEADS, m), f32)]),
        scratch_shapes=[pltpu.VMEM((D_MODEL, D_IN_PAD), jnp.bfloat16)],
        compiler_params=pltpu.CompilerParams(
            dimension_semantics=("arbitrary",), vmem_limit_bytes=VMEM_LIMIT),
    )(x2d, norm_g, w_in, hsum, q_g, k_g, b_f)


def _scan_kernel(lf_ref, tri_ref, prev_ref, c_ref):
    lf = lf_ref[...].reshape(c_ref.shape[1:])
    tri = tri_ref[...]
    within = sum(_dot(p, tri) for p in _bf16_split3(lf))
    total = jnp.broadcast_to(within[:, SCAN_CHUNK - 1:SCAN_CHUNK], within.shape)
    prev = prev_ref[...]
    offset = sum(_dot(prev, p) for p in _bf16_split3(total))
    c_ref[0] = within + offset


def _forget_scan(lf_t, b, tri, prev):
    heads, rows, c = lf_t.shape
    nc = rows // b
    r = heads * nc
    full = lambda a: pl.BlockSpec(a.shape, lambda i: (0,) * a.ndim)
    blk = pl.BlockSpec((1, r, c), lambda i: (i, 0, 0))
    return pl.pallas_call(
        _scan_kernel,
        grid=(b,),
        in_specs=[pl.BlockSpec((heads, nc, c), lambda i: (0, i, 0)), full(tri), full(prev)],
        out_specs=blk,
        out_shape=jax.ShapeDtypeStruct((b, r, c), jnp.float32),
        compiler_params=pltpu.CompilerParams(dimension_semantics=("arbitrary",)),
    )(lf_t, tri, prev)


def _lane_is_first_head(shape):
    return lax.broadcasted_iota(jnp.int32, shape, len(shape) - 1) < HEAD_DIM


def _split_heads(q):
    first = _lane_is_first_head(q.shape)
    zero = jnp.zeros_like(q)
    return jnp.where(first, q, zero), jnp.where(first, zero, q)


def _widen(col, width):
    return jnp.concatenate([col] * (width // LANES), axis=1)


def _fox_kernel(bound_ref, *refs):
    bounded = bound_ref[0, 0] <= FOX_MAX_BOUND

    @pl.when(bounded)
    def _():
        _fox_path(bound_ref, *refs, running_max=False)

    @pl.when(jnp.logical_not(bounded))
    def _():
        _fox_path(bound_ref, *refs, running_max=True)


def _fox_path(bound_ref, q_ref, k_ref, v_ref, ccol_ref, crow_ref, g_ref, o_ref,
              qs_ref, m_ref, acc_ref, ct_ref, *, running_max):
    bq, bk = ATT_BQ, ATT_BK
    qi = pl.program_id(2)
    q0, q1 = _split_heads(q_ref[0])
    qs_ref[0] = q0
    qs_ref[1] = q1
    shift = 0.0 if running_max else bound_ref[0, 0]
    c_all = ccol_ref[0]
    head_lane = lax.broadcasted_iota(jnp.int32, c_all.shape, 1)
    c_first = []
    for h in range(2):
        m_ref[h] = jnp.full((bq, LANES), NEG, jnp.float32)
        acc_ref[h] = jnp.zeros((bq, PAIR_W), jnp.float32)
        mine = head_lane == 2 * pl.program_id(1) + h
        ct = jnp.sum(jnp.where(mine, c_all, 0.0), axis=1, keepdims=True)
        c_first.append(ct[0:1, :])
        ct_ref[h] = jnp.broadcast_to(ct, (bq, LANES)) - shift

    def run(blocks, masked, heads=(0, 1)):
        chains = []
        for j, bk, r0 in blocks:
            n = bq - r0
            start = pl.multiple_of(j * bk, bk)
            k = k_ref[0, pl.ds(start, bk), :]
            v = v_ref[0, pl.ds(start, bk), :]
            first = _lane_is_first_head(v.shape)
            ones = jnp.ones_like(v)
            vs = (jnp.where(first, v, ones), jnp.where(first, ones, v))
            keep = None
            if masked:
                keep = (lax.broadcasted_iota(jnp.int32, (n, bk), 1)
                        <= lax.broadcasted_iota(jnp.int32, (n, bk), 0))
            for h in heads:
                cs = crow_ref[0, 0, h:h + 1, pl.ds(start, bk)]
                chains.append((h, slice(r0, bq), bk, k, vs[h], cs, keep))
        qk = [_dot_nt(qs_ref[h, rows, :], k) for h, rows, _, k, _, _, _ in chains]
        ps = []
        for (h, rows, bk, _, _, cs, keep), s in zip(chains, qk):
            s = s + (_widen(ct_ref[h, rows, :], bk) - cs)
            if masked:
                s = jnp.where(keep, s, NEG)
            if running_max:
                m_prev = m_ref[h, rows, :]
                m_next = jnp.maximum(m_prev, jnp.max(s, axis=1, keepdims=True))
                s = s - _widen(m_next, bk)
                acc_ref[h, rows, :] = jnp.exp(m_prev - m_next) * acc_ref[h, rows, :]
                m_ref[h, rows, :] = m_next
            ps.append(jnp.exp(s).astype(jnp.bfloat16))
        for (h, rows, _, _, v, _, _), p in zip(chains, ps):
            acc_ref[h, rows, :] = acc_ref[h, rows, :] + _dot(p, v)

    sub = bq // bk
    diagonal = [(qi * sub + jj, bk, jj * bk) for jj in range(sub)]
    if running_max:
        for blk in diagonal:
            run([blk], True)
    else:
        run(diagonal, True)

    per_trip = 1 if running_max else FOX_BLOCKS_PER_TRIP
    assert (bq // FOX_BK_FULL) % per_trip == 0
    trips = qi * (bq // FOX_BK_FULL // per_trip)

    def trip(heads):
        def body(j, carry):
            run([(j * per_trip + u, FOX_BK_FULL, 0) for u in range(per_trip)], False, heads)
            return carry
        return body

    if running_max:
        lax.fori_loop(0, trips, trip((0, 1)), 0)
    else:
        span = FOX_BK_FULL * per_trip
        pos = lax.broadcasted_iota(jnp.int32, (1, crow_ref.shape[-1]), 1)
        first_trip = []
        for h in range(2):
            bias = c_first[h] - crow_ref[0, 0, h:h + 1, :]
            dead = jnp.logical_and(bias <= -FOX_CUTOFF, pos < qi * bq)
            n_dead = jnp.sum(jnp.where(dead, 1.0, 0.0)).astype(jnp.int32)
            first_trip.append(n_dead // span)
        both = jnp.maximum(first_trip[0], first_trip[1])
        lax.fori_loop(first_trip[0], both, trip((0,)), 0)
        lax.fori_loop(first_trip[1], both, trip((1,)), 0)
        lax.fori_loop(both, trips, trip((0, 1)), 0)

    a0, a1 = acc_ref[0], acc_ref[1]
    o0 = a0 / pltpu.roll(a0, HEAD_DIM, axis=1)
    o1 = a1 / pltpu.roll(a1, HEAD_DIM, axis=1)
    o = jnp.where(_lane_is_first_head(o0.shape), o0, o1)
    o_ref[0] = (o * _silu(g_ref[0])).astype(o_ref.dtype)


def _fox_attention(bound, q, k, v, ccol, crow, gate):
    b, s, w = q.shape
    t = ATT_BQ
    pairs = w // PAIR_W
    qblk = pl.BlockSpec((1, t, PAIR_W), lambda bi, hp, qi: (bi, qi, hp))
    seq = pl.BlockSpec((1, s, PAIR_W), lambda bi, hp, qi: (bi, 0, hp))
    return pl.pallas_call(
        _fox_kernel,
        grid=(b, pairs, s // t),
        in_specs=[pl.BlockSpec(memory_space=pltpu.SMEM), qblk, seq, seq,
                  pl.BlockSpec((1, t, ccol.shape[-1]), lambda bi, hp, qi: (bi, qi, 0)),
                  pl.BlockSpec((1, 1, 2, s), lambda bi, hp, qi: (bi, hp, 0, 0)),
                  qblk],
        out_specs=qblk,
        out_shape=jax.ShapeDtypeStruct((b, s, w), jnp.bfloat16),
        scratch_shapes=[pltpu.VMEM((2, t, PAIR_W), jnp.bfloat16),
                        pltpu.VMEM((2, t, LANES), jnp.float32),
                        pltpu.VMEM((2, t, PAIR_W), jnp.float32),
                        pltpu.VMEM((2, t, LANES), jnp.float32)],
        compiler_params=pltpu.CompilerParams(
            dimension_semantics=("arbitrary", "arbitrary", "arbitrary"),
            vmem_limit_bytes=VMEM_LIMIT),
    )(bound, q, k, v, ccol, crow, gate)


def _fox_logit_bound(q_gain, k_gain):
    gq = jnp.max(jnp.abs(q_gain), axis=-1)
    gk = jnp.max(jnp.abs(k_gain), axis=-1)
    return (HEAD_DIM * SCALE * 1.01) * gq * gk + 0.05


def _sb_kernel(q_ref, k_ref, v_ref, g_ref, tri_ref, o_ref, qs_ref, r_ref, acc_ref):
    bq, t = ATT_BQ, ATT_BK
    sub = bq // t
    qi = pl.program_id(2)
    q0, q1 = _split_heads(q_ref[0])
    qs_ref[0] = q0
    qs_ref[1] = q1
    for h in range(2):
        r_ref[h] = jnp.zeros((bq, LANES), jnp.float32)
        acc_ref[h] = jnp.zeros((bq, PAIR_W), jnp.float32)

    def step(distances):
        tri = tri_ref[...]
        keep = (lax.broadcasted_iota(jnp.int32, (t, t), 1)
                < lax.broadcasted_iota(jnp.int32, (t, t), 0))
        rows = {a: slice(a * t, (a + 1) * t) for a in range(sub)}
        chains, valid, ks, vs = [], {}, {}, {}
        for n, d in enumerate(distances):
            for a in range(sub):
                j = qi * sub + a - d
                valid[n, a] = j >= 0
                start = pl.multiple_of(jnp.maximum(j, 0) * t, t)
                ks[n, a] = k_ref[0, pl.ds(start, t), :]
                vs[n, a] = v_ref[0, pl.ds(start, t), :]
                chains += [(n, a, h) for h in range(2)]
        diagonal = {n: isinstance(d, int) and d == 0 for n, d in enumerate(distances)}
        z = {(n, a, h): _dot_nt(qs_ref[h, rows[a], :], ks[n, a]) for n, a, h in chains}
        sps = {}
        for c in chains:
            sp = _softplus(z[c])
            if diagonal[c[0]]:
                sp = jnp.where(keep, sp, 0.0)
            sps[c] = sp.astype(jnp.bfloat16)
        cum = {c: _dot(sps[c], tri) for c in chains}
        w = {}
        for c in chains:
            wc = jnp.exp(z[c] - cum[c])
            if diagonal[c[0]]:
                wc = jnp.where(keep, wc, 0.0)
            w[c] = wc.astype(jnp.bfloat16)
        pv = {(n, a, h): _dot(w[n, a, h], vs[n, a]) for n, a, h in chains}
        for n, a, h in chains:
            r_prev = r_ref[h, rows[a], :]
            add = jnp.exp(-r_prev) * pv[n, a, h]
            mass = jnp.broadcast_to(cum[n, a, h][:, 0:1], (t, LANES))
            if not diagonal[n]:
                add = jnp.where(valid[n, a], add, 0.0)
                mass = jnp.where(valid[n, a], mass, 0.0)
            acc_ref[h, rows[a], :] = acc_ref[h, rows[a], :] + add
            r_ref[h, rows[a], :] = r_prev + mass

    def finished(d):
        done = jnp.bool_(True)
        for a in range(sub):
            rows = slice(a * t, (a + 1) * t)
            exhausted = qi * sub + a - (d + 1) < 0
            saturated = jnp.min(r_ref[:, rows, :]) >= SB_CUTOFF
            done = jnp.logical_and(done, jnp.logical_or(exhausted, saturated))
        return done

    step([0, 1])

    def cond(carry):
        _, done = carry
        return jnp.logical_not(done)

    def body(carry):
        d, _ = carry
        step([d])
        return d + 1, finished(d)

    lax.while_loop(cond, body, (jnp.int32(2), finished(1)))

    o = jnp.where(_lane_is_first_head((bq, PAIR_W)), acc_ref[0], acc_ref[1])
    o_ref[0] = (o * _silu(g_ref[0])).astype(o_ref.dtype)


def _sb_attention(q, k, v, gate, tri):
    b, s, w = q.shape
    t = ATT_BQ
    pairs = w // PAIR_W
    qblk = pl.BlockSpec((1, t, PAIR_W), lambda bi, hp, qi: (bi, qi, hp))
    seq = pl.BlockSpec((1, s, PAIR_W), lambda bi, hp, qi: (bi, 0, hp))
    return pl.pallas_call(
        _sb_kernel,
        grid=(b, pairs, s // t),
        in_specs=[qblk, seq, seq, qblk,
                  pl.BlockSpec(tri.shape, lambda bi, hp, qi: (0, 0))],
        out_specs=qblk,
        out_shape=jax.ShapeDtypeStruct((b, s, w), jnp.bfloat16),
        scratch_shapes=[pltpu.VMEM((2, t, PAIR_W), jnp.bfloat16),
                        pltpu.VMEM((2, t, LANES), jnp.float32),
                        pltpu.VMEM((2, t, PAIR_W), jnp.float32)],
        compiler_params=pltpu.CompilerParams(
            dimension_semantics=("arbitrary", "arbitrary", "arbitrary"),
            vmem_limit_bytes=VMEM_LIMIT),
    )(q, k, v, gate, tri)


def _out_proj_kernel(x_ref, fox_ref, sb_ref, px_ref, hist_ref, pg_ref,
                     wpool_ref, pscale_ref, wout_ref, o_ref, xp_ref, wo_ref, *, blocks_per_seq):
    bm = ROW_BLOCK
    i = pl.program_id(0)

    @pl.when(i == 0)
    def _():
        chunk = D_MIX // WEIGHT_PACK_CHUNKS
        for r in range(0, D_MIX, chunk):
            wo_ref[r:r + chunk, :] = wout_ref[r:r + chunk, :].astype(jnp.bfloat16)

    pos0 = (i % blocks_per_seq) * bm
    hist = hist_ref[...]
    xp_ref[0:MAX_WINDOW, :] = jnp.where(pos0 == 0, jnp.zeros_like(hist), hist)
    px = px_ref[...]
    xp_ref[MAX_WINDOW:MAX_WINDOW + bm, :] = px

    sums = {}
    run = px
    for d in range(1, MAX_WINDOW):
        run = run + xp_ref[MAX_WINDOW - d:MAX_WINDOW - d + bm, :]
        if d + 1 in POOL_WINDOWS:
            sums[d + 1] = run
    group = lax.broadcasted_iota(jnp.int32, (bm, POOL_W), 1) // POOL_GROUP_DIM
    pos = pos0 + lax.broadcasted_iota(jnp.int32, (bm, POOL_W), 0)
    wsum = sums[POOL_WINDOWS[-1]]
    win = jnp.full((bm, POOL_W), POOL_WINDOWS[-1], jnp.int32)
    for g in range(POOL_GROUPS - 2, -1, -1):
        wsum = jnp.where(group == g, sums[POOL_WINDOWS[g]], wsum)
        win = jnp.where(group == g, POOL_WINDOWS[g], win)
    count = jnp.minimum(pos + 1, win).astype(jnp.float32)
    pooled = wsum / count - px
    y = _dot(pooled.astype(jnp.bfloat16), wpool_ref[...]) * pscale_ref[...]
    pool_out = (y * _silu(pg_ref[...])).astype(jnp.bfloat16)

    o_ref[...] = (x_ref[...]
                  + _dot(fox_ref[...], wo_ref[0:FOX_W, :])
                  + _dot(pool_out, wo_ref[FOX_W:FOX_W + POOL_W, :])
                  + _dot(sb_ref[...], wo_ref[FOX_W + POOL_W:D_MIX, :]))


def _out_proj(layer, x2d, fox_o, sb_o, px, pg, wpool_bd, pscale, w_out, seq_len):
    m = x2d.shape[0]
    bm = ROW_BLOCK
    row = lambda w: pl.BlockSpec((bm, w), lambda i: (i, 0))
    per_layer = lambda a: _layer_spec(a, layer)
    hist_per_block = bm // MAX_WINDOW
    hist = pl.BlockSpec((MAX_WINDOW, POOL_W),
                        lambda i: (jnp.maximum(i * hist_per_block - 1, 0), 0))
    kern = functools.partial(_out_proj_kernel, blocks_per_seq=seq_len // bm)
    return pl.pallas_call(
        kern,
        grid=(m // bm,),
        in_specs=[row(D_MODEL), row(FOX_W), row(SB_W), row(POOL_W), hist, row(POOL_W),
                  per_layer(wpool_bd), per_layer(pscale),
                  _layer_spec(w_out, layer, single_buffer=True)],
        out_specs=row(D_MODEL),
        out_shape=jax.ShapeDtypeStruct((m, D_MODEL), jnp.float32),
        scratch_shapes=[pltpu.VMEM((MAX_WINDOW + bm, POOL_W), jnp.float32),
                        pltpu.VMEM((D_MIX, D_MODEL), jnp.bfloat16)],
        compiler_params=pltpu.CompilerParams(
            dimension_semantics=("arbitrary",), vmem_limit_bytes=VMEM_LIMIT),
    )(x2d, fox_o, sb_o, px, px, pg, wpool_bd, pscale, w_out)


def _constants(seq_len):
    bf16 = jnp.bfloat16
    idx = jnp.arange(MXU_TILE)
    hsum =(idx[:, None] // HEAD_DIM == idx[None, :] // HEAD_DIM).astype(bf16)
    c = jnp.arange(SCAN_CHUNK)
    scan_tri = (c[:, None] <= c[None, :]).astype(bf16)
    nc = seq_len // SCAN_CHUNK
    r = jnp.arange(FOX_HEADS * nc)
    scan_prev = ((r[:, None] // nc == r[None, :] // nc)
                 & (r[None, :] < r[:, None])).astype(bf16)
    a = jnp.arange(ATT_BK)
    sb_tri = (a[:, None] >= a[None, :]).astype(bf16)
    return hsum, scan_tri, scan_prev, sb_tri


def _block_diag(w_pool):
    out = jnp.zeros((w_pool.shape[0], POOL_W, POOL_W), w_pool.dtype)
    for g in range(POOL_GROUPS):
        lo = g * POOL_GROUP_DIM
        out = out.at[:, lo:lo + POOL_GROUP_DIM, lo:lo + POOL_GROUP_DIM].set(w_pool[:, g])
    return out.astype(jnp.bfloat16)


def kernel(x, norm_g, w_in, b_f, q_norm_g, k_norm_g, w_pool, pool_scale, w_out):
    b, s, d = x.shape
    depth = norm_g.shape[0]
    assert d == D_MODEL and s % ROW_BLOCK == 0 and s % ATT_BQ == 0 and ATT_BQ % ATT_BK == 0
    assert w_in.shape[-1] == C_FF + FOX_HEADS
    hsum, scan_tri, scan_prev, sb_tri = _constants(s)
    nc = s // SCAN_CHUNK
    x2d = x.reshape(b * s, d)
    wpool_bd = _block_diag(w_pool)
    w_in_bf = w_in.astype(jnp.bfloat16)
    rowvec = lambda a: a[:, None, :]
    norm_g3, pscale3 = rowvec(norm_g), rowvec(pool_scale)
    b_f3 = rowvec(jnp.pad(b_f, ((0, 0), (0, FF_PAD - FOX_HEADS))))
    q_g = rowvec(jnp.tile(q_norm_g, (1, FOX_HEADS)))
    k_g = rowvec(jnp.tile(k_norm_g, (1, FOX_HEADS)))
    bounds = _fox_logit_bound(q_norm_g, k_norm_g)[:, None]
    for l in range(depth):
        (fq, fk, fv, fg, px, pg, sq, sk, sv, sg, lf) = _in_proj(
            l, x2d, norm_g3, w_in_bf, hsum, q_g, k_g, b_f3)

        c = _forget_scan(lf.reshape(FOX_HEADS, b * nc, SCAN_CHUNK), b, scan_tri, scan_prev)
        crow = c.reshape(b, FOX_HEADS // 2, 2, s)
        ccol = c.reshape(b, FOX_HEADS, s).transpose(0, 2, 1)

        r3 = lambda a: a.reshape(b, s, a.shape[-1])
        fox_o = _fox_attention(bounds[l:l + 1], r3(fq), r3(fk), r3(fv), ccol, crow, r3(fg))
        sb_o = _sb_attention(r3(sq), r3(sk), r3(sv), r3(sg), sb_tri)

        x2d = _out_proj(l, x2d, fox_o.reshape(b * s, FOX_W), sb_o.reshape(b * s, SB_W),
                        px, pg, wpool_bd, pscale3, w_out, s)
    return x2d.reshape(b, s, d)
```

```python
import functools

import jax
import jax.numpy as jnp
from jax import lax
from jax.experimental import pallas as pl
from jax.experimental.pallas import tpu as pltpu

D_MODEL = 1024
HEAD_DIM = 64
FOX_HEADS = 8
SB_HEADS = 4
POOL_GROUPS = 4
POOL_WINDOWS = (2, 4, 8, 16)
POOL_GROUP_DIM = 64
FOX_W = FOX_HEADS * HEAD_DIM
SB_W = SB_HEADS * HEAD_DIM
POOL_W = POOL_GROUPS * POOL_GROUP_DIM
D_MIX = FOX_W + POOL_W + SB_W
EPS = 1e-6
NEG = -1e30
SCALE = HEAD_DIM ** -0.5
LOG2E = 1.4426950408889634
FOX_MAX_BOUND = 30.0
SB_CUTOFF = 106.0
FOX_CUTOFF = 106.0

LANES = 128
MXU_TILE = 256
PAIR_W = 2 * HEAD_DIM
MAX_WINDOW = max(POOL_WINDOWS)
FF_PAD = LANES

ROW_BLOCK = 1024
WEIGHT_PACK_CHUNKS = 8
ATT_BQ = 1024
ATT_BK = 256
FOX_BK_FULL = 512
FOX_BLOCKS_PER_TRIP = 2
SCAN_CHUNK = LANES

VMEM_LIMIT = 56 * 1024 * 1024

C_FQ, C_FK, C_FV, C_FG = 0, FOX_W, 2 * FOX_W, 3 * FOX_W
C_PX = 4 * FOX_W
C_PG = C_PX + POOL_W
C_SQ = C_PG + POOL_W
C_SK = C_SQ + SB_W
C_SV = C_SK + SB_W
C_SG = C_SV + SB_W
C_FF = C_SG + SB_W
D_IN_PAD = C_FF + FF_PAD


def _bf16_split3(x):
    a = x.astype(jnp.bfloat16)
    r = x - a.astype(jnp.float32)
    b = r.astype(jnp.bfloat16)
    c = (r - b.astype(jnp.float32)).astype(jnp.bfloat16)
    return a, b, c


def _dot(a, b):
    return jnp.dot(a, b, preferred_element_type=jnp.float32)


def _dot_nt(a, b):
    return lax.dot_general(a, b, (((1,), (1,)), ((), ())),
                           preferred_element_type=jnp.float32)


def _silu(x):
    return x * (1.0 / (1.0 + jnp.exp(-x)))


def _softplus(x):
    return jnp.maximum(x, 0.0) + jnp.log(1.0 + jnp.exp2(jnp.abs(x) * (-LOG2E)))


def _in_proj_kernel(x_ref, g_ref, win_ref, hsum_ref, qg_ref, kg_ref, bf_ref,
                    fq_ref, fk_ref, fv_ref, fg_ref, px_ref, pg_ref,
                    sq_ref, sk_ref, sv_ref, sg_ref, lf_ref, w_ref):
    @pl.when(pl.program_id(0) == 0)
    def _():
        chunk = D_MODEL // WEIGHT_PACK_CHUNKS
        for r in range(0, D_MODEL, chunk):
            rows = slice(r, r + chunk)
            w_ref[rows, 0:C_PX] = win_ref[rows, 0:C_PX].astype(jnp.bfloat16)
            w_ref[rows, C_PX:C_FF] = win_ref[rows, C_PX + FOX_HEADS:].astype(jnp.bfloat16)
            ff = win_ref[rows, C_PX:C_PX + FOX_HEADS].astype(jnp.bfloat16)
            w_ref[rows, C_FF:] = jnp.concatenate(
                [ff, jnp.zeros((chunk, FF_PAD - FOX_HEADS), jnp.bfloat16)], axis=1)

    x = x_ref[...]
    ms = jnp.mean(x * x, axis=-1, keepdims=True)
    h = (x * lax.rsqrt(ms + EPS) * g_ref[...]).astype(jnp.bfloat16)

    def proj(c0, width):
        return _dot(h, w_ref[:, c0:c0 + width])

    def head_norm(y, gain):
        sq = (y * y).astype(jnp.bfloat16)
        hsum = hsum_ref[...]
        w = hsum.shape[0]
        ssq = jnp.concatenate(
            [_dot(sq[:, c:c + w], hsum) for c in range(0, y.shape[1], w)], axis=1)
        return y * lax.rsqrt(ssq * (1.0 / HEAD_DIM) + EPS) * gain

    fq_ref[...] = (head_norm(proj(C_FQ, FOX_W), qg_ref[...]) * SCALE).astype(jnp.bfloat16)
    fk_ref[...] = head_norm(proj(C_FK, FOX_W), kg_ref[...]).astype(jnp.bfloat16)
    fv_ref[...] = proj(C_FV, FOX_W).astype(jnp.bfloat16)
    fg_ref[...] = proj(C_FG, FOX_W)
    px_ref[...] = proj(C_PX, POOL_W)
    pg_ref[...] = proj(C_PG, POOL_W)
    sq_ref[...] = (proj(C_SQ, SB_W) * SCALE).astype(jnp.bfloat16)
    sk_ref[...] = proj(C_SK, SB_W).astype(jnp.bfloat16)
    sv_ref[...] = proj(C_SV, SB_W).astype(jnp.bfloat16)
    sg_ref[...] = proj(C_SG, SB_W)
    ff = proj(C_FF, FF_PAD) + bf_ref[...]
    lf_ref[...] = (-_softplus(-ff)).T[:FOX_HEADS, :]


def _layer_spec(stacked, layer, single_buffer=False):
    tail = stacked.shape[1:]
    mode = dict(pipeline_mode=pl.Buffered(1)) if single_buffer else {}
    return pl.BlockSpec((None,) + tail, lambda i: (layer,) + (0,) * len(tail), **mode)


def _in_proj(layer, x2d, norm_g, w_in, hsum, q_g, k_g, b_f):
    m = x2d.shape[0]
    bm = ROW_BLOCK
    row = lambda w: pl.BlockSpec((bm, w), lambda i: (i, 0))
    full = lambda a: pl.BlockSpec(a.shape, lambda i: (0,) * a.ndim)
    per_layer = lambda a: _layer_spec(a, layer)
    bf16, f32 = jnp.bfloat16, jnp.float32
    outs = [(FOX_W, bf16), (FOX_W, bf16), (FOX_W, bf16), (FOX_W, f32),
            (POOL_W, f32), (POOL_W, f32),
            (SB_W, bf16), (SB_W, bf16), (SB_W, bf16), (SB_W, f32)]
    return pl.pallas_call(
        _in_proj_kernel,
        grid=(m // bm,),
        in_specs=[row(D_MODEL), per_layer(norm_g), _layer_spec(w_in, layer, single_buffer=True),
                  full(hsum), per_layer(q_g), per_layer(k_g), per_layer(b_f)],
        out_specs=[row(w) for w, _ in outs] + [pl.BlockSpec((FOX_HEADS, bm), lambda i: (0, i))],
        out_shape=([jax.ShapeDtypeStruct((m, w), dt) for w, dt in outs]
                   + [jax.ShapeDtypeStruct((FOX_HEADS, m), f32)]),
        scratch_shapes=[pltpu.VMEM((D_MODEL, D_IN_PAD), jnp.bfloat16)],
        compiler_params=pltpu.CompilerParams(
            dimension_semantics=("arbitrary",), vmem_limit_bytes=VMEM_LIMIT),
    )(x2d, norm_g, w_in, hsum, q_g, k_g, b_f)


def _scan_kernel(lf_ref, tri_ref, prev_ref, c_ref):
    lf = lf_ref[...].reshape(c_ref.shape[1:])
    tri = tri_ref[...]
    within = sum(_dot(p, tri) for p in _bf16_split3(lf))
    total = jnp.broadcast_to(within[:, SCAN_CHUNK - 1:SCAN_CHUNK], within.shape)
    prev = prev_ref[...]
    offset = sum(_dot(prev, p) for p in _bf16_split3(total))
    c_ref[0] = within + offset


def _forget_scan(lf_t, b, tri, prev):
    heads, rows, c = lf_t.shape
    nc = rows // b
    r = heads * nc
    full = lambda a: pl.BlockSpec(a.shape, lambda i: (0,) * a.ndim)
    blk = pl.BlockSpec((1, r, c), lambda i: (i, 0, 0))
    return pl.pallas_call(
        _scan_kernel,
        grid=(b,),
        in_specs=[pl.BlockSpec((heads, nc, c), lambda i: (0, i, 0)), full(tri), full(prev)],
        out_specs=blk,
        out_shape=jax.ShapeDtypeStruct((b, r, c), jnp.float32),
        compiler_params=pltpu.CompilerParams(dimension_semantics=("arbitrary",)),
    )(lf_t, tri, prev)


def _lane_is_first_head(shape):
    return lax.broadcasted_iota(jnp.int32, shape, len(shape) - 1) < HEAD_DIM


def _split_heads(q):
    first = _lane_is_first_head(q.shape)
    zero = jnp.zeros_like(q)
    return jnp.where(first, q, zero), jnp.where(first, zero, q)


def _widen(col, width):
    return jnp.concatenate([col] * (width // LANES), axis=1)


def _fox_kernel(bound_ref, *refs):
    bounded = bound_ref[0, 0] <= FOX_MAX_BOUND

    @pl.when(bounded)
    def _():
        _fox_path(bound_ref, *refs, running_max=False)

    @pl.when(jnp.logical_not(bounded))
    def _():
        _fox_path(bound_ref, *refs, running_max=True)


def _fox_path(bound_ref, q_ref, k_ref, v_ref, ccol_ref, crow_ref, g_ref, o_ref,
              qs_ref, m_ref, acc_ref, ct_ref, *, running_max):
    bq, bk = ATT_BQ, ATT_BK
    qi = pl.program_id(2)
    q0, q1 = _split_heads(q_ref[0])
    qs_ref[0] = q0
    qs_ref[1] = q1
    shift = 0.0 if running_max else bound_ref[0, 0]
    c_all = ccol_ref[0]
    head_lane = lax.broadcasted_iota(jnp.int32, c_all.shape, 1)
    c_first = []
    for h in range(2):
        m_ref[h] = jnp.full((bq, LANES), NEG, jnp.float32)
        acc_ref[h] = jnp.zeros((bq, PAIR_W), jnp.float32)
        mine = head_lane == 2 * pl.program_id(1) + h
        ct = jnp.sum(jnp.where(mine, c_all, 0.0), axis=1, keepdims=True)
        c_first.append(ct[0:1, :])
        ct_ref[h] = jnp.broadcast_to(ct, (bq, LANES)) - shift

    def run(blocks, masked, heads=(0, 1)):
        chains = []
        for j, bk, r0 in blocks:
            n = bq - r0
            start = pl.multiple_of(j * bk, bk)
            k = k_ref[0, pl.ds(start, bk), :]
            v = v_ref[0, pl.ds(start, bk), :]
            first = _lane_is_first_head(v.shape)
            ones = jnp.ones_like(v)
            vs = (jnp.where(first, v, ones), jnp.where(first, ones, v))
            keep = None
            if masked:
                keep = (lax.broadcasted_iota(jnp.int32, (n, bk), 1)
                        <= lax.broadcasted_iota(jnp.int32, (n, bk), 0))
            for h in heads:
                cs = crow_ref[0, 0, h:h + 1, pl.ds(start, bk)]
                chains.append((h, slice(r0, bq), bk, k, vs[h], cs, keep))
        qk = [_dot_nt(qs_ref[h, rows, :], k) for h, rows, _, k, _, _, _ in chains]
        ps = []
        for (h, rows, bk, _, _, cs, keep), s in zip(chains, qk):
            s = s + (_widen(ct_ref[h, rows, :], bk) - cs)
            if masked:
                s = jnp.where(keep, s, NEG)
            if running_max:
                m_prev = m_ref[h, rows, :]
                m_next = jnp.maximum(m_prev, jnp.max(s, axis=1, keepdims=True))
                s = s - _widen(m_next, bk)
                acc_ref[h, rows, :] = jnp.exp(m_prev - m_next) * acc_ref[h, rows, :]
                m_ref[h, rows, :] = m_next
            ps.append(jnp.exp(s).astype(jnp.bfloat16))
        for (h, rows, _, _, v, _, _), p in zip(chains, ps):
            acc_ref[h, rows, :] = acc_ref[h, rows, :] + _dot(p, v)

    sub = bq // bk
    diagonal = [(qi * sub + jj, bk, jj * bk) for jj in range(sub)]
    if running_max:
        for blk in diagonal:
            run([blk], True)
    else:
        run(diagonal, True)

    per_trip = 1 if running_max else FOX_BLOCKS_PER_TRIP
    assert (bq // FOX_BK_FULL) % per_trip == 0
    trips = qi * (bq // FOX_BK_FULL // per_trip)

    def trip(heads):
        def body(j, carry):
            run([(j * per_trip + u, FOX_BK_FULL, 0) for u in range(per_trip)], False, heads)
            return carry
        return body

    if running_max:
        lax.fori_loop(0, trips, trip((0, 1)), 0)
    else:
        span = FOX_BK_FULL * per_trip
        pos = lax.broadcasted_iota(jnp.int32, (1, crow_ref.shape[-1]), 1)
        first_trip = []
        for h in range(2):
            bias = c_first[h] - crow_ref[0, 0, h:h + 1, :]
            dead = jnp.logical_and(bias <= -FOX_CUTOFF, pos < qi * bq)
            n_dead = jnp.sum(jnp.where(dead, 1.0, 0.0)).astype(jnp.int32)
            first_trip.append(n_dead // span)
        both = jnp.maximum(first_trip[0], first_trip[1])
        lax.fori_loop(first_trip[0], both, trip((0,)), 0)
        lax.fori_loop(first_trip[1], both, trip((1,)), 0)
        lax.fori_loop(both, trips, trip((0, 1)), 0)

    a0, a1 = acc_ref[0], acc_ref[1]
    first = _lane_is_first_head(a0.shape)
    values = jnp.where(first, a0, a1)
    sums = pltpu.roll(jnp.where(first, a1, a0), HEAD_DIM, axis=1)
    o_ref[0] = (values / sums * _silu(g_ref[0])).astype(o_ref.dtype)


def _fox_attention(bound, q, k, v, ccol, crow, gate):
    b, s, w = q.shape
    t = ATT_BQ
    pairs = w // PAIR_W
    qblk = pl.BlockSpec((1, t, PAIR_W), lambda bi, hp, qi: (bi, qi, hp))
    seq = pl.BlockSpec((1, s, PAIR_W), lambda bi, hp, qi: (bi, 0, hp))
    return pl.pallas_call(
        _fox_kernel,
        grid=(b, pairs, s // t),
        in_specs=[pl.BlockSpec(memory_space=pltpu.SMEM), qblk, seq, seq,
                  pl.BlockSpec((1, t, ccol.shape[-1]), lambda bi, hp, qi: (bi, qi, 0)),
                  pl.BlockSpec((1, 1, 2, s), lambda bi, hp, qi: (bi, hp, 0, 0)),
                  qblk],
        out_specs=qblk,
        out_shape=jax.ShapeDtypeStruct((b, s, w), jnp.bfloat16),
        scratch_shapes=[pltpu.VMEM((2, t, PAIR_W), jnp.bfloat16),
                        pltpu.VMEM((2, t, LANES), jnp.float32),
                        pltpu.VMEM((2, t, PAIR_W), jnp.float32),
                        pltpu.VMEM((2, t, LANES), jnp.float32)],
        compiler_params=pltpu.CompilerParams(
            dimension_semantics=("arbitrary", "arbitrary", "arbitrary"),
            vmem_limit_bytes=VMEM_LIMIT),
    )(bound, q, k, v, ccol, crow, gate)


def _fox_logit_bound(q_gain, k_gain):
    gq = jnp.max(jnp.abs(q_gain), axis=-1)
    gk = jnp.max(jnp.abs(k_gain), axis=-1)
    return (HEAD_DIM * SCALE * 1.01) * gq * gk + 0.05


def _sb_kernel(q_ref, k_ref, v_ref, g_ref, tri_ref, o_ref, qs_ref, r_ref, acc_ref):
    bq, t = ATT_BQ, ATT_BK
    sub = bq // t
    qi = pl.program_id(2)
    q0, q1 = _split_heads(q_ref[0])
    qs_ref[0] = q0
    qs_ref[1] = q1
    for h in range(2):
        r_ref[h] = jnp.zeros((bq, LANES), jnp.float32)
        acc_ref[h] = jnp.zeros((bq, PAIR_W), jnp.float32)

    def step(distances):
        tri = tri_ref[...]
        keep = (lax.broadcasted_iota(jnp.int32, (t, t), 1)
                < lax.broadcasted_iota(jnp.int32, (t, t), 0))
        rows = {a: slice(a * t, (a + 1) * t) for a in range(sub)}
        chains, valid, ks, vs = [], {}, {}, {}
        for n, d in enumerate(distances):
            for a in range(sub):
                j = qi * sub + a - d
                valid[n, a] = j >= 0
                start = pl.multiple_of(jnp.maximum(j, 0) * t, t)
                ks[n, a] = k_ref[0, pl.ds(start, t), :]
                vs[n, a] = v_ref[0, pl.ds(start, t), :]
                chains += [(n, a, h) for h in range(2)]
        diagonal = {n: isinstance(d, int) and d == 0 for n, d in enumerate(distances)}
        z = {(n, a, h): _dot_nt(qs_ref[h, rows[a], :], ks[n, a]) for n, a, h in chains}
        sps = {}
        for c in chains:
            sp = _softplus(z[c])
            if diagonal[c[0]]:
                sp = jnp.where(keep, sp, 0.0)
            sps[c] = sp.astype(jnp.bfloat16)
        cum = {c: _dot(sps[c], tri) for c in chains}
        w = {}
        for c in chains:
            wc = jnp.exp(z[c] - cum[c])
            if diagonal[c[0]]:
                wc = jnp.where(keep, wc, 0.0)
            w[c] = wc.astype(jnp.bfloat16)
        pv = {(n, a, h): _dot(w[n, a, h], vs[n, a]) for n, a, h in chains}
        for n, a, h in chains:
            r_prev = r_ref[h, rows[a], :]
            add = jnp.exp(-r_prev) * pv[n, a, h]
            mass = jnp.broadcast_to(cum[n, a, h][:, 0:1], (t, LANES))
            if not diagonal[n]:
                add = jnp.where(valid[n, a], add, 0.0)
                mass = jnp.where(valid[n, a], mass, 0.0)
            acc_ref[h, rows[a], :] = acc_ref[h, rows[a], :] + add
            r_ref[h, rows[a], :] = r_prev + mass

    def finished(d):
        done = jnp.bool_(True)
        for a in range(sub):
            rows = slice(a * t, (a + 1) * t)
            exhausted = qi * sub + a - (d + 1) < 0
            saturated = jnp.min(r_ref[:, rows, :]) >= SB_CUTOFF
            done = jnp.logical_and(done, jnp.logical_or(exhausted, saturated))
        return done

    step([0, 1])

    def cond(carry):
        _, done = carry
        return jnp.logical_not(done)

    def body(carry):
        d, _ = carry
        step([d])
        return d + 1, finished(d)

    lax.while_loop(cond, body, (jnp.int32(2), finished(1)))

    o = jnp.where(_lane_is_first_head((bq, PAIR_W)), acc_ref[0], acc_ref[1])
    o_ref[0] = (o * _silu(g_ref[0])).astype(o_ref.dtype)


def _sb_attention(q, k, v, gate, tri):
    b, s, w = q.shape
    t = ATT_BQ
    pairs = w // PAIR_W
    qblk = pl.BlockSpec((1, t, PAIR_W), lambda bi, hp, qi: (bi, qi, hp))
    seq = pl.BlockSpec((1, s, PAIR_W), lambda bi, hp, qi: (bi, 0, hp))
    return pl.pallas_call(
        _sb_kernel,
        grid=(b, pairs, s // t),
        in_specs=[qblk, seq, seq, qblk,
                  pl.BlockSpec(tri.shape, lambda bi, hp, qi: (0, 0))],
        out_specs=qblk,
        out_shape=jax.ShapeDtypeStruct((b, s, w), jnp.bfloat16),
        scratch_shapes=[pltpu.VMEM((2, t, PAIR_W), jnp.bfloat16),
                        pltpu.VMEM((2, t, LANES), jnp.float32),
                        pltpu.VMEM((2, t, PAIR_W), jnp.float32)],
        compiler_params=pltpu.CompilerParams(
            dimension_semantics=("arbitrary", "arbitrary", "arbitrary"),
            vmem_limit_bytes=VMEM_LIMIT),
    )(q, k, v, gate, tri)


def _out_proj_kernel(x_ref, fox_ref, sb_ref, px_ref, hist_ref, pg_ref,
                     wpool_ref, pscale_ref, wout_ref, o_ref, xp_ref, wo_ref, *, blocks_per_seq):
    bm = ROW_BLOCK
    i = pl.program_id(0)

    @pl.when(i == 0)
    def _():
        chunk = D_MIX // WEIGHT_PACK_CHUNKS
        for r in range(0, D_MIX, chunk):
            wo_ref[r:r + chunk, :] = wout_ref[r:r + chunk, :].astype(jnp.bfloat16)

    pos0 = (i % blocks_per_seq) * bm
    hist = hist_ref[...]
    xp_ref[0:MAX_WINDOW, :] = jnp.where(pos0 == 0, jnp.zeros_like(hist), hist)
    px = px_ref[...]
    xp_ref[MAX_WINDOW:MAX_WINDOW + bm, :] = px

    sums = {}
    run = px
    for d in range(1, MAX_WINDOW):
        run = run + xp_ref[MAX_WINDOW - d:MAX_WINDOW - d + bm, :]
        if d + 1 in POOL_WINDOWS:
            sums[d + 1] = run
    group = lax.broadcasted_iota(jnp.int32, (bm, POOL_W), 1) // POOL_GROUP_DIM
    pos = pos0 + lax.broadcasted_iota(jnp.int32, (bm, POOL_W), 0)
    wsum = sums[POOL_WINDOWS[-1]]
    win = jnp.full((bm, POOL_W), POOL_WINDOWS[-1], jnp.int32)
    for g in range(POOL_GROUPS - 2, -1, -1):
        wsum = jnp.where(group == g, sums[POOL_WINDOWS[g]], wsum)
        win = jnp.where(group == g, POOL_WINDOWS[g], win)
    count = jnp.minimum(pos + 1, win).astype(jnp.float32)
    pooled = wsum / count - px
    y = _dot(pooled.astype(jnp.bfloat16), wpool_ref[...]) * pscale_ref[...]
    pool_out = (y * _silu(pg_ref[...])).astype(jnp.bfloat16)

    o_ref[...] = (x_ref[...]
                  + _dot(fox_ref[...], wo_ref[0:FOX_W, :])
                  + _dot(pool_out, wo_ref[FOX_W:FOX_W + POOL_W, :])
                  + _dot(sb_ref[...], wo_ref[FOX_W + POOL_W:D_MIX, :]))


def _out_proj(layer, x2d, fox_o, sb_o, px, pg, wpool_bd, pscale, w_out, seq_len):
    m = x2d.shape[0]
    bm = ROW_BLOCK
    row = lambda w: pl.BlockSpec((bm, w), lambda i: (i, 0))
    per_layer = lambda a: _layer_spec(a, layer)
    hist_per_block = bm // MAX_WINDOW
    hist = pl.BlockSpec((MAX_WINDOW, POOL_W),
                        lambda i: (jnp.maximum(i * hist_per_block - 1, 0), 0))
    kern = functools.partial(_out_proj_kernel, blocks_per_seq=seq_len // bm)
    return pl.pallas_call(
        kern,
        grid=(m // bm,),
        in_specs=[row(D_MODEL), row(FOX_W), row(SB_W), row(POOL_W), hist, row(POOL_W),
                  per_layer(wpool_bd), per_layer(pscale),
                  _layer_spec(w_out, layer, single_buffer=True)],
        out_specs=row(D_MODEL),
        out_shape=jax.ShapeDtypeStruct((m, D_MODEL), jnp.float32),
        scratch_shapes=[pltpu.VMEM((MAX_WINDOW + bm, POOL_W), jnp.float32),
                        pltpu.VMEM((D_MIX, D_MODEL), jnp.bfloat16)],
        compiler_params=pltpu.CompilerParams(
            dimension_semantics=("arbitrary",), vmem_limit_bytes=VMEM_LIMIT),
    )(x2d, fox_o, sb_o, px, px, pg, wpool_bd, pscale, w_out)


def _constants(seq_len):
    bf16 = jnp.bfloat16
    idx = jnp.arange(MXU_TILE)
    hsum =(idx[:, None] // HEAD_DIM == idx[None, :] // HEAD_DIM).astype(bf16)
    c = jnp.arange(SCAN_CHUNK)
    scan_tri = (c[:, None] <= c[None, :]).astype(bf16)
    nc = seq_len // SCAN_CHUNK
    r = jnp.arange(FOX_HEADS * nc)
    scan_prev = ((r[:, None] // nc == r[None, :] // nc)
                 & (r[None, :] < r[:, None])).astype(bf16)
    a = jnp.arange(ATT_BK)
    sb_tri = (a[:, None] >= a[None, :]).astype(bf16)
    return hsum, scan_tri, scan_prev, sb_tri


def _block_diag(w_pool):
    out = jnp.zeros((w_pool.shape[0], POOL_W, POOL_W), w_pool.dtype)
    for g in range(POOL_GROUPS):
        lo = g * POOL_GROUP_DIM
        out = out.at[:, lo:lo + POOL_GROUP_DIM, lo:lo + POOL_GROUP_DIM].set(w_pool[:, g])
    return out.astype(jnp.bfloat16)


def kernel(x, norm_g, w_in, b_f, q_norm_g, k_norm_g, w_pool, pool_scale, w_out):
    b, s, d = x.shape
    depth = norm_g.shape[0]
    assert d == D_MODEL and s % ROW_BLOCK == 0 and s % ATT_BQ == 0 and ATT_BQ % ATT_BK == 0
    assert w_in.shape[-1] == C_FF + FOX_HEADS
    hsum, scan_tri, scan_prev, sb_tri = _constants(s)
    nc = s // SCAN_CHUNK
    x2d = x.reshape(b * s, d)
    wpool_bd = _block_diag(w_pool)
    w_in_bf = w_in.astype(jnp.bfloat16)
    rowvec = lambda a: a[:, None, :]
    norm_g3, pscale3 = rowvec(norm_g), rowvec(pool_scale)
    b_f3 = rowvec(jnp.pad(b_f, ((0, 0), (0, FF_PAD - FOX_HEADS))))
    q_g = rowvec(jnp.tile(q_norm_g, (1, FOX_HEADS)))
    k_g = rowvec(jnp.tile(k_norm_g, (1, FOX_HEADS)))
    bounds = _fox_logit_bound(q_norm_g, k_norm_g)[:, None]
    for l in range(depth):
        (fq, fk, fv, fg, px, pg, sq, sk, sv, sg, lf) = _in_proj(
            l, x2d, norm_g3, w_in_bf, hsum, q_g, k_g, b_f3)

        c = _forget_scan(lf.reshape(FOX_HEADS, b * nc, SCAN_CHUNK), b, scan_tri, scan_prev)
        crow = c.reshape(b, FOX_HEADS // 2, 2, s)
        ccol = c.reshape(b, FOX_HEADS, s).transpose(0, 2, 1)

        r3 = lambda a: a.reshape(b, s, a.shape[-1])
        fox_o = _fox_attention(bounds[l:l + 1], r3(fq), r3(fk), r3(fv), ccol, crow, r3(fg))
        sb_o = _sb_attention(r3(sq), r3(sk), r3(sv), r3(sg), sb_tri)

        x2d = _out_proj(l, x2d, fox_o.reshape(b * s, FOX_W), sb_o.reshape(b * s, SB_W),
                        px, pg, wpool_bd, pscale3, w_out, s)
    return x2d.reshape(b, s, d)
```

```python
import functools

import jax
import jax.numpy as jnp
from jax import lax
from jax.experimental import pallas as pl
from jax.experimental.pallas import tpu as pltpu

D_MODEL = 1024
HEAD_DIM = 64
FOX_HEADS = 8
SB_HEADS = 4
POOL_GROUPS = 4
POOL_WINDOWS = (2, 4, 8, 16)
POOL_GROUP_DIM = 64
FOX_W = FOX_HEADS * HEAD_DIM
SB_W = SB_HEADS * HEAD_DIM
POOL_W = POOL_GROUPS * POOL_GROUP_DIM
D_MIX = FOX_W + POOL_W + SB_W
EPS = 1e-6
NEG = -1e30
SCALE = HEAD_DIM ** -0.5
LOG2E = 1.4426950408889634
FOX_MAX_BOUND = 30.0
SB_CUTOFF = 106.0
FOX_CUTOFF = 106.0

LANES = 128
MXU_TILE = 256
PAIR_W = 2 * HEAD_DIM
MAX_WINDOW = max(POOL_WINDOWS)
FF_PAD = LANES

ROW_BLOCK = 1024
WEIGHT_PACK_CHUNKS = 8
ATT_BQ = 1024
SB_BQ = 2048
ATT_BK = 256
FOX_BK_FULL = 512
FOX_BLOCKS_PER_TRIP = 2
SCAN_CHUNK = LANES

VMEM_LIMIT = 56 * 1024 * 1024

C_FQ, C_FK, C_FV, C_FG = 0, FOX_W, 2 * FOX_W, 3 * FOX_W
C_PX = 4 * FOX_W
C_PG = C_PX + POOL_W
C_SQ = C_PG + POOL_W
C_SK = C_SQ + SB_W
C_SV = C_SK + SB_W
C_SG = C_SV + SB_W
C_FF = C_SG + SB_W
D_IN_PAD = C_FF + FF_PAD


def _bf16_split3(x):
    a = x.astype(jnp.bfloat16)
    r = x - a.astype(jnp.float32)
    b = r.astype(jnp.bfloat16)
    c = (r - b.astype(jnp.float32)).astype(jnp.bfloat16)
    return a, b, c


def _dot(a, b):
    return jnp.dot(a, b, preferred_element_type=jnp.float32)


def _dot_nt(a, b):
    return lax.dot_general(a, b, (((1,), (1,)), ((), ())),
                           preferred_element_type=jnp.float32)


def _silu(x):
    return x * (1.0 / (1.0 + jnp.exp(-x)))


def _softplus(x):
    return jnp.maximum(x, 0.0) + jnp.log(1.0 + jnp.exp2(jnp.abs(x) * (-LOG2E)))


def _in_proj_kernel(x_ref, g_ref, win_ref, hsum_ref, qg_ref, kg_ref, bf_ref,
                    fq_ref, fk_ref, fv_ref, fg_ref, px_ref, pg_ref,
                    sq_ref, sk_ref, sv_ref, sg_ref, lf_ref, w_ref):
    @pl.when(pl.program_id(0) == 0)
    def _():
        chunk = D_MODEL // WEIGHT_PACK_CHUNKS
        for r in range(0, D_MODEL, chunk):
            rows = slice(r, r + chunk)
            w_ref[rows, 0:C_PX] = win_ref[rows, 0:C_PX].astype(jnp.bfloat16)
            w_ref[rows, C_PX:C_FF] = win_ref[rows, C_PX + FOX_HEADS:].astype(jnp.bfloat16)
            ff = win_ref[rows, C_PX:C_PX + FOX_HEADS].astype(jnp.bfloat16)
            w_ref[rows, C_FF:] = jnp.concatenate(
                [ff, jnp.zeros((chunk, FF_PAD - FOX_HEADS), jnp.bfloat16)], axis=1)

    x = x_ref[...]
    ms = jnp.mean(x * x, axis=-1, keepdims=True)
    h = (x * lax.rsqrt(ms + EPS) * g_ref[...]).astype(jnp.bfloat16)

    def proj(c0, width):
        return _dot(h, w_ref[:, c0:c0 + width])

    def head_norm(y, gain):
        sq = (y * y).astype(jnp.bfloat16)
        hsum = hsum_ref[...]
        w = hsum.shape[0]
        ssq = jnp.concatenate(
            [_dot(sq[:, c:c + w], hsum) for c in range(0, y.shape[1], w)], axis=1)
        return y * lax.rsqrt(ssq * (1.0 / HEAD_DIM) + EPS) * gain

    fq_ref[...] = (head_norm(proj(C_FQ, FOX_W), qg_ref[...]) * SCALE).astype(jnp.bfloat16)
    fk_ref[...] = head_norm(proj(C_FK, FOX_W), kg_ref[...]).astype(jnp.bfloat16)
    fv_ref[...] = proj(C_FV, FOX_W).astype(jnp.bfloat16)
    fg_ref[...] = proj(C_FG, FOX_W)
    px_ref[...] = proj(C_PX, POOL_W)
    pg_ref[...] = proj(C_PG, POOL_W)
    sq_ref[...] = (proj(C_SQ, SB_W) * SCALE).astype(jnp.bfloat16)
    sk_ref[...] = proj(C_SK, SB_W).astype(jnp.bfloat16)
    sv_ref[...] = proj(C_SV, SB_W).astype(jnp.bfloat16)
    sg_ref[...] = proj(C_SG, SB_W)
    ff = proj(C_FF, FF_PAD) + bf_ref[...]
    lf_ref[...] = (-_softplus(-ff)).T[:FOX_HEADS, :]


def _layer_spec(stacked, layer, single_buffer=False):
    tail = stacked.shape[1:]
    mode = dict(pipeline_mode=pl.Buffered(1)) if single_buffer else {}
    return pl.BlockSpec((None,) + tail, lambda i: (layer,) + (0,) * len(tail), **mode)


def _in_proj(layer, x2d, norm_g, w_in, hsum, q_g, k_g, b_f):
    m = x2d.shape[0]
    bm = ROW_BLOCK
    row = lambda w: pl.BlockSpec((bm, w), lambda i: (i, 0))
    full = lambda a: pl.BlockSpec(a.shape, lambda i: (0,) * a.ndim)
    per_layer = lambda a: _layer_spec(a, layer)
    bf16, f32 = jnp.bfloat16, jnp.float32
    outs = [(FOX_W, bf16), (FOX_W, bf16), (FOX_W, bf16), (FOX_W, f32),
            (POOL_W, f32), (POOL_W, f32),
            (SB_W, bf16), (SB_W, bf16), (SB_W, bf16), (SB_W, f32)]
    return pl.pallas_call(
        _in_proj_kernel,
        grid=(m // bm,),
        in_specs=[row(D_MODEL), per_layer(norm_g), _layer_spec(w_in, layer, single_buffer=True),
                  full(hsum), per_layer(q_g), per_layer(k_g), per_layer(b_f)],
        out_specs=[row(w) for w, _ in outs] + [pl.BlockSpec((FOX_HEADS, bm), lambda i: (0, i))],
        out_shape=([jax.ShapeDtypeStruct((m, w), dt) for w, dt in outs]
                   + [jax.ShapeDtypeStruct((FOX_HEADS, m), f32)]),
        scratch_shapes=[pltpu.VMEM((D_MODEL, D_IN_PAD), jnp.bfloat16)],
        compiler_params=pltpu.CompilerParams(
            dimension_semantics=("arbitrary",), vmem_limit_bytes=VMEM_LIMIT),
    )(x2d, norm_g, w_in, hsum, q_g, k_g, b_f)


def _scan_kernel(lf_ref, tri_ref, prev_ref, c_ref):
    lf = lf_ref[...].reshape(c_ref.shape[1:])
    tri = tri_ref[...]
    within = sum(_dot(p, tri) for p in _bf16_split3(lf))
    total = jnp.broadcast_to(within[:, SCAN_CHUNK - 1:SCAN_CHUNK], within.shape)
    prev = prev_ref[...]
    offset = sum(_dot(prev, p) for p in _bf16_split3(total))
    c_ref[0] = within + offset


def _forget_scan(lf_t, b, tri, prev):
    heads, rows, c = lf_t.shape
    nc = rows // b
    r = heads * nc
    full = lambda a: pl.BlockSpec(a.shape, lambda i: (0,) * a.ndim)
    blk = pl.BlockSpec((1, r, c), lambda i: (i, 0, 0))
    return pl.pallas_call(
        _scan_kernel,
        grid=(b,),
        in_specs=[pl.BlockSpec((heads, nc, c), lambda i: (0, i, 0)), full(tri), full(prev)],
        out_specs=blk,
        out_shape=jax.ShapeDtypeStruct((b, r, c), jnp.float32),
        compiler_params=pltpu.CompilerParams(dimension_semantics=("arbitrary",)),
    )(lf_t, tri, prev)


def _lane_is_first_head(shape):
    return lax.broadcasted_iota(jnp.int32, shape, len(shape) - 1) < HEAD_DIM


def _split_heads(q):
    first = _lane_is_first_head(q.shape)
    zero = jnp.zeros_like(q)
    return jnp.where(first, q, zero), jnp.where(first, zero, q)


def _widen(col, width):
    return jnp.concatenate([col] * (width // LANES), axis=1)


def _fox_kernel(bound_ref, *refs):
    bounded = bound_ref[0, 0] <= FOX_MAX_BOUND

    @pl.when(bounded)
    def _():
        _fox_path(bound_ref, *refs, running_max=False)

    @pl.when(jnp.logical_not(bounded))
    def _():
        _fox_path(bound_ref, *refs, running_max=True)


def _fox_path(bound_ref, q_ref, k_ref, v_ref, ccol_ref, crow_ref, g_ref, o_ref,
              qs_ref, m_ref, acc_ref, ct_ref, *, running_max):
    bq, bk = ATT_BQ, ATT_BK
    qi = pl.program_id(2)
    q0, q1 = _split_heads(q_ref[0])
    qs_ref[0] = q0
    qs_ref[1] = q1
    shift = 0.0 if running_max else bound_ref[0, 0]
    c_all = ccol_ref[0]
    head_lane = lax.broadcasted_iota(jnp.int32, c_all.shape, 1)
    c_first = []
    for h in range(2):
        m_ref[h] = jnp.full((bq, LANES), NEG, jnp.float32)
        acc_ref[h] = jnp.zeros((bq, PAIR_W), jnp.float32)
        mine = head_lane == 2 * pl.program_id(1) + h
        ct = jnp.sum(jnp.where(mine, c_all, 0.0), axis=1, keepdims=True)
        c_first.append(ct[0:1, :])
        ct_ref[h] = jnp.broadcast_to(ct, (bq, LANES)) - shift

    def run(blocks, masked, heads=(0, 1)):
        chains = []
        for j, bk, r0 in blocks:
            n = bq - r0
            start = pl.multiple_of(j * bk, bk)
            k = k_ref[0, pl.ds(start, bk), :]
            v = v_ref[0, pl.ds(start, bk), :]
            first = _lane_is_first_head(v.shape)
            ones = jnp.ones_like(v)
            vs = (jnp.where(first, v, ones), jnp.where(first, ones, v))
            keep = None
            if masked:
                keep = (lax.broadcasted_iota(jnp.int32, (n, bk), 1)
                        <= lax.broadcasted_iota(jnp.int32, (n, bk), 0))
            for h in heads:
                cs = crow_ref[0, 0, h:h + 1, pl.ds(start, bk)]
                chains.append((h, slice(r0, bq), bk, k, vs[h], cs, keep))
        qk = [_dot_nt(qs_ref[h, rows, :], k) for h, rows, _, k, _, _, _ in chains]
        ps = []
        for (h, rows, bk, _, _, cs, keep), s in zip(chains, qk):
            s = s + (_widen(ct_ref[h, rows, :], bk) - cs)
            if masked:
                s = jnp.where(keep, s, NEG)
            if running_max:
                m_prev = m_ref[h, rows, :]
                m_next = jnp.maximum(m_prev, jnp.max(s, axis=1, keepdims=True))
                s = s - _widen(m_next, bk)
                acc_ref[h, rows, :] = jnp.exp(m_prev - m_next) * acc_ref[h, rows, :]
                m_ref[h, rows, :] = m_next
            ps.append(jnp.exp(s).astype(jnp.bfloat16))
        for (h, rows, _, _, v, _, _), p in zip(chains, ps):
            acc_ref[h, rows, :] = acc_ref[h, rows, :] + _dot(p, v)

    sub = bq // bk
    diagonal = [(qi * sub + jj, bk, jj * bk) for jj in range(sub)]
    if running_max:
        for blk in diagonal:
            run([blk], True)
    else:
        run(diagonal, True)

    per_trip = 1 if running_max else FOX_BLOCKS_PER_TRIP
    assert (bq // FOX_BK_FULL) % per_trip == 0
    trips = qi * (bq // FOX_BK_FULL // per_trip)

    def trip(heads):
        def body(j, carry):
            run([(j * per_trip + u, FOX_BK_FULL, 0) for u in range(per_trip)], False, heads)
            return carry
        return body

    if running_max:
        lax.fori_loop(0, trips, trip((0, 1)), 0)
    else:
        span = FOX_BK_FULL * per_trip
        pos = lax.broadcasted_iota(jnp.int32, (1, crow_ref.shape[-1]), 1)
        first_trip = []
        for h in range(2):
            bias = c_first[h] - crow_ref[0, 0, h:h + 1, :]
            dead = jnp.logical_and(bias <= -FOX_CUTOFF, pos < qi * bq)
            n_dead = jnp.sum(jnp.where(dead, 1.0, 0.0)).astype(jnp.int32)
            first_trip.append(n_dead // span)
        both = jnp.maximum(first_trip[0], first_trip[1])
        lax.fori_loop(first_trip[0], both, trip((0,)), 0)
        lax.fori_loop(first_trip[1], both, trip((1,)), 0)

        def double_trip(j, carry):
            base = (both + 2 * j) * per_trip
            run([(base + u, FOX_BK_FULL, 0) for u in range(2 * per_trip)], False)
            return carry

        n_double = (trips - both) // 2
        lax.fori_loop(0, n_double, double_trip, 0)
        lax.fori_loop(both + 2 * n_double, trips, trip((0, 1)), 0)

    a0, a1 = acc_ref[0], acc_ref[1]
    first = _lane_is_first_head(a0.shape)
    values = jnp.where(first, a0, a1)
    sums = pltpu.roll(jnp.where(first, a1, a0), HEAD_DIM, axis=1)
    o_ref[0] = (values / sums * _silu(g_ref[0])).astype(o_ref.dtype)


def _fox_attention(bound, q, k, v, ccol, crow, gate):
    b, s, w = q.shape
    t = ATT_BQ
    pairs = w // PAIR_W
    qblk = pl.BlockSpec((1, t, PAIR_W), lambda bi, hp, qi: (bi, qi, hp))
    seq = pl.BlockSpec((1, s, PAIR_W), lambda bi, hp, qi: (bi, 0, hp))
    return pl.pallas_call(
        _fox_kernel,
        grid=(b, pairs, s // t),
        in_specs=[pl.BlockSpec(memory_space=pltpu.SMEM), qblk, seq, seq,
                  pl.BlockSpec((1, t, ccol.shape[-1]), lambda bi, hp, qi: (bi, qi, 0)),
                  pl.BlockSpec((1, 1, 2, s), lambda bi, hp, qi: (bi, hp, 0, 0)),
                  qblk],
        out_specs=qblk,
        out_shape=jax.ShapeDtypeStruct((b, s, w), jnp.bfloat16),
        scratch_shapes=[pltpu.VMEM((2, t, PAIR_W), jnp.bfloat16),
                        pltpu.VMEM((2, t, LANES), jnp.float32),
                        pltpu.VMEM((2, t, PAIR_W), jnp.float32),
                        pltpu.VMEM((2, t, LANES), jnp.float32)],
        compiler_params=pltpu.CompilerParams(
            dimension_semantics=("arbitrary", "arbitrary", "arbitrary"),
            vmem_limit_bytes=VMEM_LIMIT),
    )(bound, q, k, v, ccol, crow, gate)


def _fox_logit_bound(q_gain, k_gain):
    gq = jnp.max(jnp.abs(q_gain), axis=-1)
    gk = jnp.max(jnp.abs(k_gain), axis=-1)
    return (HEAD_DIM * SCALE * 1.01) * gq * gk + 0.05


def _sb_kernel(q_ref, k_ref, v_ref, g_ref, tri_ref, o_ref, qs_ref, r_ref, acc_ref):
    bq, t = SB_BQ, ATT_BK
    sub = bq // t
    qi = pl.program_id(2)
    q0, q1 = _split_heads(q_ref[0])
    qs_ref[0] = q0
    qs_ref[1] = q1
    for h in range(2):
        r_ref[h] = jnp.zeros((bq, LANES), jnp.float32)
        acc_ref[h] = jnp.zeros((bq, PAIR_W), jnp.float32)

    def step(distances):
        tri = tri_ref[...]
        keep = (lax.broadcasted_iota(jnp.int32, (t, t), 1)
                < lax.broadcasted_iota(jnp.int32, (t, t), 0))
        rows = {a: slice(a * t, (a + 1) * t) for a in range(sub)}
        chains, valid, ks, vs = [], {}, {}, {}
        for n, d in enumerate(distances):
            for a in range(sub):
                j = qi * sub + a - d
                valid[n, a] = j >= 0
                start = pl.multiple_of(jnp.maximum(j, 0) * t, t)
                ks[n, a] = k_ref[0, pl.ds(start, t), :]
                vs[n, a] = v_ref[0, pl.ds(start, t), :]
                chains += [(n, a, h) for h in range(2)]
        diagonal = {n: isinstance(d, int) and d == 0 for n, d in enumerate(distances)}
        z = {(n, a, h): _dot_nt(qs_ref[h, rows[a], :], ks[n, a]) for n, a, h in chains}
        sps = {}
        for c in chains:
            sp = _softplus(z[c])
            if diagonal[c[0]]:
                sp = jnp.where(keep, sp, 0.0)
            sps[c] = sp.astype(jnp.bfloat16)
        cum = {c: _dot(sps[c], tri) for c in chains}
        w = {}
        for c in chains:
            wc = jnp.exp(z[c] - cum[c])
            if diagonal[c[0]]:
                wc = jnp.where(keep, wc, 0.0)
            w[c] = wc.astype(jnp.bfloat16)
        pv = {(n, a, h): _dot(w[n, a, h], vs[n, a]) for n, a, h in chains}
        for n, a, h in chains:
            r_prev = r_ref[h, rows[a], :]
            add = jnp.exp(-r_prev) * pv[n, a, h]
            mass = jnp.broadcast_to(cum[n, a, h][:, 0:1], (t, LANES))
            if not diagonal[n]:
                add = jnp.where(valid[n, a], add, 0.0)
                mass = jnp.where(valid[n, a], mass, 0.0)
            acc_ref[h, rows[a], :] = acc_ref[h, rows[a], :] + add
            r_ref[h, rows[a], :] = r_prev + mass

    def finished(d):
        done = jnp.bool_(True)
        for a in range(sub):
            rows = slice(a * t, (a + 1) * t)
            exhausted = qi * sub + a - (d + 1) < 0
            saturated = jnp.min(r_ref[:, rows, :]) >= SB_CUTOFF
            done = jnp.logical_and(done, jnp.logical_or(exhausted, saturated))
        return done

    step([0, 1])

    def cond(carry):
        _, done = carry
        return jnp.logical_not(done)

    def body(carry):
        d, _ = carry
        step([d])
        return d + 1, finished(d)

    lax.while_loop(cond, body, (jnp.int32(2), finished(1)))

    o = jnp.where(_lane_is_first_head((bq, PAIR_W)), acc_ref[0], acc_ref[1])
    o_ref[0] = (o * _silu(g_ref[0])).astype(o_ref.dtype)


def _sb_attention(q, k, v, gate, tri):
    b, s, w = q.shape
    t = SB_BQ
    pairs = w // PAIR_W
    qblk = pl.BlockSpec((1, t, PAIR_W), lambda bi, hp, qi: (bi, qi, hp))
    seq = pl.BlockSpec((1, s, PAIR_W), lambda bi, hp, qi: (bi, 0, hp))
    return pl.pallas_call(
        _sb_kernel,
        grid=(b, pairs, s // t),
        in_specs=[qblk, seq, seq, qblk,
                  pl.BlockSpec(tri.shape, lambda bi, hp, qi: (0, 0))],
        out_specs=qblk,
        out_shape=jax.ShapeDtypeStruct((b, s, w), jnp.bfloat16),
        scratch_shapes=[pltpu.VMEM((2, t, PAIR_W), jnp.bfloat16),
                        pltpu.VMEM((2, t, LANES), jnp.float32),
                        pltpu.VMEM((2, t, PAIR_W), jnp.float32)],
        compiler_params=pltpu.CompilerParams(
            dimension_semantics=("arbitrary", "arbitrary", "arbitrary"),
            vmem_limit_bytes=VMEM_LIMIT),
    )(q, k, v, gate, tri)


def _out_proj_kernel(x_ref, fox_ref, sb_ref, px_ref, hist_ref, pg_ref,
                     wpool_ref, pscale_ref, wout_ref, o_ref, xp_ref, wo_ref, *, blocks_per_seq):
    bm = ROW_BLOCK
    i = pl.program_id(0)

    @pl.when(i == 0)
    def _():
        chunk = D_MIX // WEIGHT_PACK_CHUNKS
        for r in range(0, D_MIX, chunk):
            wo_ref[r:r + chunk, :] = wout_ref[r:r + chunk, :].astype(jnp.bfloat16)

    pos0 = (i % blocks_per_seq) * bm
    hist = hist_ref[...]
    xp_ref[0:MAX_WINDOW, :] = jnp.where(pos0 == 0, jnp.zeros_like(hist), hist)
    px = px_ref[...]
    xp_ref[MAX_WINDOW:MAX_WINDOW + bm, :] = px

    sums = {}
    run = px
    for d in range(1, MAX_WINDOW):
        run = run + xp_ref[MAX_WINDOW - d:MAX_WINDOW - d + bm, :]
        if d + 1 in POOL_WINDOWS:
            sums[d + 1] = run
    group = lax.broadcasted_iota(jnp.int32, (bm, POOL_W), 1) // POOL_GROUP_DIM
    pos = pos0 + lax.broadcasted_iota(jnp.int32, (bm, POOL_W), 0)
    wsum = sums[POOL_WINDOWS[-1]]
    win = jnp.full((bm, POOL_W), POOL_WINDOWS[-1], jnp.int32)
    for g in range(POOL_GROUPS - 2, -1, -1):
        wsum = jnp.where(group == g, sums[POOL_WINDOWS[g]], wsum)
        win = jnp.where(group == g, POOL_WINDOWS[g], win)
    count = jnp.minimum(pos + 1, win).astype(jnp.float32)
    pooled = wsum / count - px
    y = _dot(pooled.astype(jnp.bfloat16), wpool_ref[...]) * pscale_ref[...]
    pool_out = (y * _silu(pg_ref[...])).astype(jnp.bfloat16)

    o_ref[...] = (x_ref[...]
                  + _dot(fox_ref[...], wo_ref[0:FOX_W, :])
                  + _dot(pool_out, wo_ref[FOX_W:FOX_W + POOL_W, :])
                  + _dot(sb_ref[...], wo_ref[FOX_W + POOL_W:D_MIX, :]))


def _out_proj(layer, x2d, fox_o, sb_o, px, pg, wpool_bd, pscale, w_out, seq_len):
    m = x2d.shape[0]
    bm = ROW_BLOCK
    row = lambda w: pl.BlockSpec((bm, w), lambda i: (i, 0))
    per_layer = lambda a: _layer_spec(a, layer)
    hist_per_block = bm // MAX_WINDOW
    hist = pl.BlockSpec((MAX_WINDOW, POOL_W),
                        lambda i: (jnp.maximum(i * hist_per_block - 1, 0), 0))
    kern = functools.partial(_out_proj_kernel, blocks_per_seq=seq_len // bm)
    return pl.pallas_call(
        kern,
        grid=(m // bm,),
        in_specs=[row(D_MODEL), row(FOX_W), row(SB_W), row(POOL_W), hist, row(POOL_W),
                  per_layer(wpool_bd), per_layer(pscale),
                  _layer_spec(w_out, layer, single_buffer=True)],
        out_specs=row(D_MODEL),
        out_shape=jax.ShapeDtypeStruct((m, D_MODEL), jnp.float32),
        scratch_shapes=[pltpu.VMEM((MAX_WINDOW + bm, POOL_W), jnp.float32),
                        pltpu.VMEM((D_MIX, D_MODEL), jnp.bfloat16)],
        compiler_params=pltpu.CompilerParams(
            dimension_semantics=("arbitrary",), vmem_limit_bytes=VMEM_LIMIT),
    )(x2d, fox_o, sb_o, px, px, pg, wpool_bd, pscale, w_out)


def _constants(seq_len):
    bf16 = jnp.bfloat16
    idx = jnp.arange(MXU_TILE)
    hsum =(idx[:, None] // HEAD_DIM == idx[None, :] // HEAD_DIM).astype(bf16)
    c = jnp.arange(SCAN_CHUNK)
    scan_tri = (c[:, None] <= c[None, :]).astype(bf16)
    nc = seq_len // SCAN_CHUNK
    r = jnp.arange(FOX_HEADS * nc)
    scan_prev = ((r[:, None] // nc == r[None, :] // nc)
                 & (r[None, :] < r[:, None])).astype(bf16)
    a = jnp.arange(ATT_BK)
    sb_tri = (a[:, None] >= a[None, :]).astype(bf16)
    return hsum, scan_tri, scan_prev, sb_tri


def _block_diag(w_pool):
    out = jnp.zeros((w_pool.shape[0], POOL_W, POOL_W), w_pool.dtype)
    for g in range(POOL_GROUPS):
        lo = g * POOL_GROUP_DIM
        out = out.at[:, lo:lo + POOL_GROUP_DIM, lo:lo + POOL_GROUP_DIM].set(w_pool[:, g])
    return out.astype(jnp.bfloat16)


def kernel(x, norm_g, w_in, b_f, q_norm_g, k_norm_g, w_pool, pool_scale, w_out):
    b, s, d = x.shape
    depth = norm_g.shape[0]
    assert d == D_MODEL and s % ROW_BLOCK == 0
    assert s % ATT_BQ == 0 and ATT_BQ % ATT_BK == 0 and s % SB_BQ == 0 and SB_BQ % ATT_BK == 0
    assert w_in.shape[-1] == C_FF + FOX_HEADS
    hsum, scan_tri, scan_prev, sb_tri = _constants(s)
    nc = s // SCAN_CHUNK
    x2d = x.reshape(b * s, d)
    wpool_bd = _block_diag(w_pool)
    w_in_bf = w_in.astype(jnp.bfloat16)
    rowvec = lambda a: a[:, None, :]
    norm_g3, pscale3 = rowvec(norm_g), rowvec(pool_scale)
    b_f3 = rowvec(jnp.pad(b_f, ((0, 0), (0, FF_PAD - FOX_HEADS))))
    q_g = rowvec(jnp.tile(q_norm_g, (1, FOX_HEADS)))
    k_g = rowvec(jnp.tile(k_norm_g, (1, FOX_HEADS)))
    bounds = _fox_logit_bound(q_norm_g, k_norm_g)[:, None]
    for l in range(depth):
        (fq, fk, fv, fg, px, pg, sq, sk, sv, sg, lf) = _in_proj(
            l, x2d, norm_g3, w_in_bf, hsum, q_g, k_g, b_f3)

        c = _forget_scan(lf.reshape(FOX_HEADS, b * nc, SCAN_CHUNK), b, scan_tri, scan_prev)
        crow = c.reshape(b, FOX_HEADS // 2, 2, s)
        ccol = c.reshape(b, FOX_HEADS, s).transpose(0, 2, 1)

        r3 = lambda a: a.reshape(b, s, a.shape[-1])
        fox_o = _fox_attention(bounds[l:l + 1], r3(fq), r3(fk), r3(fv), ccol, crow, r3(fg))
        sb_o = _sb_attention(r3(sq), r3(sk), r3(sv), r3(sg), sb_tri)

        x2d = _out_proj(l, x2d, fox_o.reshape(b * s, FOX_W), sb_o.reshape(b * s, SB_W),
                        px, pg, wpool_bd, pscale3, w_out, s)
    return x2d.reshape(b, s, d)
```

```python
import functools

import jax
import jax.numpy as jnp
from jax import lax
from jax.experimental import pallas as pl
from jax.experimental.pallas import tpu as pltpu

D_MODEL = 1024
HEAD_DIM = 64
FOX_HEADS = 8
SB_HEADS = 4
POOL_GROUPS = 4
POOL_WINDOWS = (2, 4, 8, 16)
POOL_GROUP_DIM = 64
FOX_W = FOX_HEADS * HEAD_DIM
SB_W = SB_HEADS * HEAD_DIM
POOL_W = POOL_GROUPS * POOL_GROUP_DIM
D_MIX = FOX_W + POOL_W + SB_W
EPS = 1e-6
NEG = -1e30
SCALE = HEAD_DIM ** -0.5
LOG2E = 1.4426950408889634
FOX_MAX_BOUND = 30.0
SB_CUTOFF = 106.0
FOX_CUTOFF = 106.0

LANES = 128
MXU_TILE = 256
PAIR_W = 2 * HEAD_DIM
MAX_WINDOW = max(POOL_WINDOWS)
FF_PAD = LANES

ROW_BLOCK = 1024
WEIGHT_PACK_CHUNKS = 8
ATT_BQ = 1024
SB_BQ = 2048
ATT_BK = 256
FOX_BK_FULL = 512
FOX_BLOCKS_PER_TRIP = 2
SCAN_CHUNK = LANES

VMEM_LIMIT = 56 * 1024 * 1024

C_FQ, C_FK, C_FV, C_FG = 0, FOX_W, 2 * FOX_W, 3 * FOX_W
C_PX = 4 * FOX_W
C_PG = C_PX + POOL_W
C_SQ = C_PG + POOL_W
C_SK = C_SQ + SB_W
C_SV = C_SK + SB_W
C_SG = C_SV + SB_W
C_FF = C_SG + SB_W
D_IN_PAD = C_FF + FF_PAD


def _bf16_split3(x):
    a = x.astype(jnp.bfloat16)
    r = x - a.astype(jnp.float32)
    b = r.astype(jnp.bfloat16)
    c = (r - b.astype(jnp.float32)).astype(jnp.bfloat16)
    return a, b, c


def _dot(a, b):
    return jnp.dot(a, b, preferred_element_type=jnp.float32)


def _dot_nt(a, b):
    return lax.dot_general(a, b, (((1,), (1,)), ((), ())),
                           preferred_element_type=jnp.float32)


def _silu(x):
    return x * (1.0 / (1.0 + jnp.exp(-x)))


def _softplus(x):
    return jnp.maximum(x, 0.0) + jnp.log(1.0 + jnp.exp2(jnp.abs(x) * (-LOG2E)))


def _in_proj_kernel(x_ref, g_ref, win_ref, hsum_ref, qg_ref, kg_ref, bf_ref,
                    fq_ref, fk_ref, fv_ref, fg_ref, px_ref, pg_ref,
                    sq_ref, sk_ref, sv_ref, sg_ref, lf_ref, w_ref):
    @pl.when(pl.program_id(0) == 0)
    def _():
        chunk = D_MODEL // WEIGHT_PACK_CHUNKS
        for r in range(0, D_MODEL, chunk):
            rows = slice(r, r + chunk)
            w_ref[rows, 0:C_PX] = win_ref[rows, 0:C_PX].astype(jnp.bfloat16)
            w_ref[rows, C_PX:C_FF] = win_ref[rows, C_PX + FOX_HEADS:].astype(jnp.bfloat16)
            ff = win_ref[rows, C_PX:C_PX + FOX_HEADS].astype(jnp.bfloat16)
            w_ref[rows, C_FF:] = jnp.concatenate(
                [ff, jnp.zeros((chunk, FF_PAD - FOX_HEADS), jnp.bfloat16)], axis=1)

    x = x_ref[...]
    ms = jnp.mean(x * x, axis=-1, keepdims=True)
    h = (x * lax.rsqrt(ms + EPS) * g_ref[...]).astype(jnp.bfloat16)

    def proj(c0, width):
        return _dot(h, w_ref[:, c0:c0 + width])

    def head_norm(y, gain):
        sq = (y * y).astype(jnp.bfloat16)
        hsum = hsum_ref[...]
        w = hsum.shape[0]
        ssq = jnp.concatenate(
            [_dot(sq[:, c:c + w], hsum) for c in range(0, y.shape[1], w)], axis=1)
        return y * lax.rsqrt(ssq * (1.0 / HEAD_DIM) + EPS) * gain

    fq_ref[...] = (head_norm(proj(C_FQ, FOX_W), qg_ref[...]) * SCALE).astype(jnp.bfloat16)
    fk_ref[...] = head_norm(proj(C_FK, FOX_W), kg_ref[...]).astype(jnp.bfloat16)
    fv_ref[...] = proj(C_FV, FOX_W).astype(jnp.bfloat16)
    fg_ref[...] = proj(C_FG, FOX_W)
    px_ref[...] = proj(C_PX, POOL_W)
    pg_ref[...] = proj(C_PG, POOL_W)
    sq_ref[...] = (proj(C_SQ, SB_W) * SCALE).astype(jnp.bfloat16)
    sk_ref[...] = proj(C_SK, SB_W).astype(jnp.bfloat16)
    sv_ref[...] = proj(C_SV, SB_W).astype(jnp.bfloat16)
    sg_ref[...] = proj(C_SG, SB_W)
    ff = proj(C_FF, FF_PAD) + bf_ref[...]
    lf_ref[...] = (-_softplus(-ff)).T[:FOX_HEADS, :]


def _layer_spec(stacked, layer, single_buffer=False):
    tail = stacked.shape[1:]
    mode = dict(pipeline_mode=pl.Buffered(1)) if single_buffer else {}
    return pl.BlockSpec((None,) + tail, lambda i: (layer,) + (0,) * len(tail), **mode)


def _in_proj(layer, x2d, norm_g, w_in, hsum, q_g, k_g, b_f):
    m = x2d.shape[0]
    bm = ROW_BLOCK
    row = lambda w: pl.BlockSpec((bm, w), lambda i: (i, 0))
    full = lambda a: pl.BlockSpec(a.shape, lambda i: (0,) * a.ndim)
    per_layer = lambda a: _layer_spec(a, layer)
    bf16, f32 = jnp.bfloat16, jnp.float32
    outs = [(FOX_W, bf16), (FOX_W, bf16), (FOX_W, bf16), (FOX_W, f32),
            (POOL_W, f32), (POOL_W, f32),
            (SB_W, bf16), (SB_W, bf16), (SB_W, bf16), (SB_W, f32)]
    return pl.pallas_call(
        _in_proj_kernel,
        grid=(m // bm,),
        in_specs=[row(D_MODEL), per_layer(norm_g), _layer_spec(w_in, layer, single_buffer=True),
                  full(hsum), per_layer(q_g), per_layer(k_g), per_layer(b_f)],
        out_specs=[row(w) for w, _ in outs] + [pl.BlockSpec((FOX_HEADS, bm), lambda i: (0, i))],
        out_shape=([jax.ShapeDtypeStruct((m, w), dt) for w, dt in outs]
                   + [jax.ShapeDtypeStruct((FOX_HEADS, m), f32)]),
        scratch_shapes=[pltpu.VMEM((D_MODEL, D_IN_PAD), jnp.bfloat16)],
        compiler_params=pltpu.CompilerParams(
            dimension_semantics=("arbitrary",), vmem_limit_bytes=VMEM_LIMIT),
    )(x2d, norm_g, w_in, hsum, q_g, k_g, b_f)


def _scan_kernel(lf_ref, tri_ref, prev_ref, c_ref):
    lf = lf_ref[...].reshape(c_ref.shape[1:])
    tri = tri_ref[...]
    within = sum(_dot(p, tri) for p in _bf16_split3(lf))
    total = jnp.broadcast_to(within[:, SCAN_CHUNK - 1:SCAN_CHUNK], within.shape)
    prev = prev_ref[...]
    offset = sum(_dot(prev, p) for p in _bf16_split3(total))
    c_ref[0] = within + offset


def _forget_scan(lf_t, b, tri, prev):
    heads, rows, c = lf_t.shape
    nc = rows // b
    r = heads * nc
    full = lambda a: pl.BlockSpec(a.shape, lambda i: (0,) * a.ndim)
    blk = pl.BlockSpec((1, r, c), lambda i: (i, 0, 0))
    return pl.pallas_call(
        _scan_kernel,
        grid=(b,),
        in_specs=[pl.BlockSpec((heads, nc, c), lambda i: (0, i, 0)), full(tri), full(prev)],
        out_specs=blk,
        out_shape=jax.ShapeDtypeStruct((b, r, c), jnp.float32),
        compiler_params=pltpu.CompilerParams(dimension_semantics=("arbitrary",)),
    )(lf_t, tri, prev)


def _lane_is_first_head(shape):
    return lax.broadcasted_iota(jnp.int32, shape, len(shape) - 1) < HEAD_DIM


def _split_heads(q):
    first = _lane_is_first_head(q.shape)
    zero = jnp.zeros_like(q)
    return jnp.where(first, q, zero), jnp.where(first, zero, q)


def _widen(col, width):
    return jnp.concatenate([col] * (width // LANES), axis=1)


def _fox_kernel(bound_ref, *refs):
    bounded = bound_ref[0, 0] <= FOX_MAX_BOUND

    @pl.when(bounded)
    def _():
        _fox_path(bound_ref, *refs, running_max=False)

    @pl.when(jnp.logical_not(bounded))
    def _():
        _fox_path(bound_ref, *refs, running_max=True)


def _fox_path(bound_ref, q_ref, k_ref, v_ref, ccol_ref, crow_ref, g_ref, o_ref,
              qs_ref, m_ref, acc_ref, ct_ref, *, running_max):
    bq, bk = ATT_BQ, ATT_BK
    qi = pl.program_id(2)
    q0, q1 = _split_heads(q_ref[0])
    qs_ref[0] = q0
    qs_ref[1] = q1
    shift = 0.0 if running_max else bound_ref[0, 0]
    c_all = ccol_ref[0]
    head_lane = lax.broadcasted_iota(jnp.int32, c_all.shape, 1)
    c_first = []
    for h in range(2):
        m_ref[h] = jnp.full((bq, LANES), NEG, jnp.float32)
        acc_ref[h] = jnp.zeros((bq, PAIR_W), jnp.float32)
        mine = head_lane == 2 * pl.program_id(1) + h
        ct = jnp.sum(jnp.where(mine, c_all, 0.0), axis=1, keepdims=True)
        c_first.append(ct[0:1, :])
        ct_ref[h] = jnp.broadcast_to(ct, (bq, LANES)) - shift

    def run(blocks, masked, heads=(0, 1)):
        chains = []
        for j, bk, r0 in blocks:
            n = bq - r0
            start = pl.multiple_of(j * bk, bk)
            k = k_ref[0, pl.ds(start, bk), :]
            v = v_ref[0, pl.ds(start, bk), :]
            first = _lane_is_first_head(v.shape)
            ones = jnp.ones_like(v)
            vs = (jnp.where(first, v, ones), jnp.where(first, ones, v))
            keep = None
            if masked:
                keep = (lax.broadcasted_iota(jnp.int32, (n, bk), 1)
                        <= lax.broadcasted_iota(jnp.int32, (n, bk), 0))
            for h in heads:
                cs = crow_ref[0, 0, h:h + 1, pl.ds(start, bk)]
                chains.append((h, slice(r0, bq), bk, k, vs[h], cs, keep))
        qk = [_dot_nt(qs_ref[h, rows, :], k) for h, rows, _, k, _, _, _ in chains]
        ps = []
        for (h, rows, bk, _, _, cs, keep), s in zip(chains, qk):
            s = s + (_widen(ct_ref[h, rows, :], bk) - cs)
            if masked:
                s = jnp.where(keep, s, NEG)
            if running_max:
                m_prev = m_ref[h, rows, :]
                m_next = jnp.maximum(m_prev, jnp.max(s, axis=1, keepdims=True))
                s = s - _widen(m_next, bk)
                acc_ref[h, rows, :] = jnp.exp(m_prev - m_next) * acc_ref[h, rows, :]
                m_ref[h, rows, :] = m_next
            ps.append(jnp.exp(s).astype(jnp.bfloat16))
        for (h, rows, _, _, v, _, _), p in zip(chains, ps):
            acc_ref[h, rows, :] = acc_ref[h, rows, :] + _dot(p, v)

    sub = bq // bk
    diagonal = [(qi * sub + jj, bk, jj * bk) for jj in range(sub)]
    if running_max:
        for blk in diagonal:
            run([blk], True)
    else:
        run(diagonal, True)

    n_blocks = qi * (bq // FOX_BK_FULL)

    def span(first, count, n, heads):
        def body(j, carry):
            run([(first + j * n + u, FOX_BK_FULL, 0) for u in range(n)], False, heads)
            return carry
        lax.fori_loop(0, count, body, 0)

    if running_max:
        span(0, n_blocks, 1, (0, 1))
    else:
        pos = lax.broadcasted_iota(jnp.int32, (1, crow_ref.shape[-1]), 1)
        live_from = []
        for h in range(2):
            bias = c_first[h] - crow_ref[0, 0, h:h + 1, :]
            dead = jnp.logical_and(bias <= -FOX_CUTOFF, pos < qi * bq)
            n_dead = jnp.sum(jnp.where(dead, 1.0, 0.0)).astype(jnp.int32)
            live_from.append(n_dead // FOX_BK_FULL)
        both_from = jnp.maximum(live_from[0], live_from[1])
        pair = FOX_BLOCKS_PER_TRIP
        for h in range(2):
            alone = both_from - live_from[h]
            span(live_from[h], alone % pair, 1, (h,))
            span(live_from[h] + alone % pair, alone // pair, pair, (h,))
        shared = n_blocks - both_from
        span(both_from, shared % pair, 1, (0, 1))
        start = both_from + shared % pair
        span(start, (shared // pair) % 2, pair, (0, 1))
        span(start + ((shared // pair) % 2) * pair, shared // (2 * pair), 2 * pair, (0, 1))

    a0, a1 = acc_ref[0], acc_ref[1]
    first = _lane_is_first_head(a0.shape)
    values = jnp.where(first, a0, a1)
    sums = pltpu.roll(jnp.where(first, a1, a0), HEAD_DIM, axis=1)
    o_ref[0] = (values / sums * _silu(g_ref[0])).astype(o_ref.dtype)


def _fox_attention(bound, q, k, v, ccol, crow, gate):
    b, s, w = q.shape
    t = ATT_BQ
    pairs = w // PAIR_W
    qblk = pl.BlockSpec((1, t, PAIR_W), lambda bi, hp, qi: (bi, qi, hp))
    seq = pl.BlockSpec((1, s, PAIR_W), lambda bi, hp, qi: (bi, 0, hp))
    return pl.pallas_call(
        _fox_kernel,
        grid=(b, pairs, s // t),
        in_specs=[pl.BlockSpec(memory_space=pltpu.SMEM), qblk, seq, seq,
                  pl.BlockSpec((1, t, ccol.shape[-1]), lambda bi, hp, qi: (bi, qi, 0)),
                  pl.BlockSpec((1, 1, 2, s), lambda bi, hp, qi: (bi, hp, 0, 0)),
                  qblk],
        out_specs=qblk,
        out_shape=jax.ShapeDtypeStruct((b, s, w), jnp.bfloat16),
        scratch_shapes=[pltpu.VMEM((2, t, PAIR_W), jnp.bfloat16),
                        pltpu.VMEM((2, t, LANES), jnp.float32),
                        pltpu.VMEM((2, t, PAIR_W), jnp.float32),
                        pltpu.VMEM((2, t, LANES), jnp.float32)],
        compiler_params=pltpu.CompilerParams(
            dimension_semantics=("arbitrary", "arbitrary", "arbitrary"),
            vmem_limit_bytes=VMEM_LIMIT),
    )(bound, q, k, v, ccol, crow, gate)


def _fox_logit_bound(q_gain, k_gain):
    gq = jnp.max(jnp.abs(q_gain), axis=-1)
    gk = jnp.max(jnp.abs(k_gain), axis=-1)
    return (HEAD_DIM * SCALE * 1.01) * gq * gk + 0.05


def _sb_kernel(q_ref, k_ref, v_ref, g_ref, tri_ref, o_ref, qs_ref, r_ref, acc_ref):
    bq, t = SB_BQ, ATT_BK
    sub = bq // t
    qi = pl.program_id(2)
    q0, q1 = _split_heads(q_ref[0])
    qs_ref[0] = q0
    qs_ref[1] = q1
    for h in range(2):
        r_ref[h] = jnp.zeros((bq, LANES), jnp.float32)
        acc_ref[h] = jnp.zeros((bq, PAIR_W), jnp.float32)

    def step(distances):
        tri = tri_ref[...]
        keep = (lax.broadcasted_iota(jnp.int32, (t, t), 1)
                < lax.broadcasted_iota(jnp.int32, (t, t), 0))
        rows = {a: slice(a * t, (a + 1) * t) for a in range(sub)}
        chains, valid, ks, vs = [], {}, {}, {}
        for n, d in enumerate(distances):
            for a in range(sub):
                j = qi * sub + a - d
                valid[n, a] = j >= 0
                start = pl.multiple_of(jnp.maximum(j, 0) * t, t)
                ks[n, a] = k_ref[0, pl.ds(start, t), :]
                vs[n, a] = v_ref[0, pl.ds(start, t), :]
                chains += [(n, a, h) for h in range(2)]
        diagonal = {n: isinstance(d, int) and d == 0 for n, d in enumerate(distances)}
        z = {(n, a, h): _dot_nt(qs_ref[h, rows[a], :], ks[n, a]) for n, a, h in chains}
        sps = {}
        for c in chains:
            sp = _softplus(z[c])
            if diagonal[c[0]]:
                sp = jnp.where(keep, sp, 0.0)
            sps[c] = sp.astype(jnp.bfloat16)
        cum = {c: _dot(sps[c], tri) for c in chains}
        w = {}
        for c in chains:
            wc = jnp.exp(z[c] - cum[c])
            if diagonal[c[0]]:
                wc = jnp.where(keep, wc, 0.0)
            w[c] = wc.astype(jnp.bfloat16)
        pv = {(n, a, h): _dot(w[n, a, h], vs[n, a]) for n, a, h in chains}
        for n, a, h in chains:
            r_prev = r_ref[h, rows[a], :]
            add = jnp.exp(-r_prev) * pv[n, a, h]
            mass = jnp.broadcast_to(cum[n, a, h][:, 0:1], (t, LANES))
            if not diagonal[n]:
                add = jnp.where(valid[n, a], add, 0.0)
                mass = jnp.where(valid[n, a], mass, 0.0)
            acc_ref[h, rows[a], :] = acc_ref[h, rows[a], :] + add
            r_ref[h, rows[a], :] = r_prev + mass

    def finished(d):
        done = jnp.bool_(True)
        for a in range(sub):
            rows = slice(a * t, (a + 1) * t)
            exhausted = qi * sub + a - (d + 1) < 0
            saturated = jnp.min(r_ref[:, rows, :]) >= SB_CUTOFF
            done = jnp.logical_and(done, jnp.logical_or(exhausted, saturated))
        return done

    step([0, 1])

    def cond(carry):
        _, done = carry
        return jnp.logical_not(done)

    def body(carry):
        d, _ = carry
        step([d])
        return d + 1, finished(d)

    lax.while_loop(cond, body, (jnp.int32(2), finished(1)))

    o = jnp.where(_lane_is_first_head((bq, PAIR_W)), acc_ref[0], acc_ref[1])
    o_ref[0] = (o * _silu(g_ref[0])).astype(o_ref.dtype)


def _sb_attention(q, k, v, gate, tri):
    b, s, w = q.shape
    t = SB_BQ
    pairs = w // PAIR_W
    qblk = pl.BlockSpec((1, t, PAIR_W), lambda bi, hp, qi: (bi, qi, hp))
    seq = pl.BlockSpec((1, s, PAIR_W), lambda bi, hp, qi: (bi, 0, hp))
    return pl.pallas_call(
        _sb_kernel,
        grid=(b, pairs, s // t),
        in_specs=[qblk, seq, seq, qblk,
                  pl.BlockSpec(tri.shape, lambda bi, hp, qi: (0, 0))],
        out_specs=qblk,
        out_shape=jax.ShapeDtypeStruct((b, s, w), jnp.bfloat16),
        scratch_shapes=[pltpu.VMEM((2, t, PAIR_W), jnp.bfloat16),
                        pltpu.VMEM((2, t, LANES), jnp.float32),
                        pltpu.VMEM((2, t, PAIR_W), jnp.float32)],
        compiler_params=pltpu.CompilerParams(
            dimension_semantics=("arbitrary", "arbitrary", "arbitrary"),
            vmem_limit_bytes=VMEM_LIMIT),
    )(q, k, v, gate, tri)


def _out_proj_kernel(x_ref, fox_ref, sb_ref, px_ref, hist_ref, pg_ref,
                     wpool_ref, pscale_ref, wout_ref, o_ref, xp_ref, wo_ref, *, blocks_per_seq):
    bm = ROW_BLOCK
    i = pl.program_id(0)

    @pl.when(i == 0)
    def _():
        chunk = D_MIX // WEIGHT_PACK_CHUNKS
        for r in range(0, D_MIX, chunk):
            wo_ref[r:r + chunk, :] = wout_ref[r:r + chunk, :].astype(jnp.bfloat16)

    pos0 = (i % blocks_per_seq) * bm
    hist = hist_ref[...]
    xp_ref[0:MAX_WINDOW, :] = jnp.where(pos0 == 0, jnp.zeros_like(hist), hist)
    px = px_ref[...]
    xp_ref[MAX_WINDOW:MAX_WINDOW + bm, :] = px

    sums = {}
    run = px
    for d in range(1, MAX_WINDOW):
        run = run + xp_ref[MAX_WINDOW - d:MAX_WINDOW - d + bm, :]
        if d + 1 in POOL_WINDOWS:
            sums[d + 1] = run
    group = lax.broadcasted_iota(jnp.int32, (bm, POOL_W), 1) // POOL_GROUP_DIM
    pos = pos0 + lax.broadcasted_iota(jnp.int32, (bm, POOL_W), 0)
    wsum = sums[POOL_WINDOWS[-1]]
    win = jnp.full((bm, POOL_W), POOL_WINDOWS[-1], jnp.int32)
    for g in range(POOL_GROUPS - 2, -1, -1):
        wsum = jnp.where(group == g, sums[POOL_WINDOWS[g]], wsum)
        win = jnp.where(group == g, POOL_WINDOWS[g], win)
    count = jnp.minimum(pos + 1, win).astype(jnp.float32)
    pooled = wsum / count - px
    y = _dot(pooled.astype(jnp.bfloat16), wpool_ref[...]) * pscale_ref[...]
    pool_out = (y * _silu(pg_ref[...])).astype(jnp.bfloat16)

    o_ref[...] = (x_ref[...]
                  + _dot(fox_ref[...], wo_ref[0:FOX_W, :])
                  + _dot(pool_out, wo_ref[FOX_W:FOX_W + POOL_W, :])
                  + _dot(sb_ref[...], wo_ref[FOX_W + POOL_W:D_MIX, :]))


def _out_proj(layer, x2d, fox_o, sb_o, px, pg, wpool_bd, pscale, w_out, seq_len):
    m = x2d.shape[0]
    bm = ROW_BLOCK
    row = lambda w: pl.BlockSpec((bm, w), lambda i: (i, 0))
    per_layer = lambda a: _layer_spec(a, layer)
    hist_per_block = bm // MAX_WINDOW
    hist = pl.BlockSpec((MAX_WINDOW, POOL_W),
                        lambda i: (jnp.maximum(i * hist_per_block - 1, 0), 0))
    kern = functools.partial(_out_proj_kernel, blocks_per_seq=seq_len // bm)
    return pl.pallas_call(
        kern,
        grid=(m // bm,),
        in_specs=[row(D_MODEL), row(FOX_W), row(SB_W), row(POOL_W), hist, row(POOL_W),
                  per_layer(wpool_bd), per_layer(pscale),
                  _layer_spec(w_out, layer, single_buffer=True)],
        out_specs=row(D_MODEL),
        out_shape=jax.ShapeDtypeStruct((m, D_MODEL), jnp.float32),
        scratch_shapes=[pltpu.VMEM((MAX_WINDOW + bm, POOL_W), jnp.float32),
                        pltpu.VMEM((D_MIX, D_MODEL), jnp.bfloat16)],
        compiler_params=pltpu.CompilerParams(
            dimension_semantics=("arbitrary",), vmem_limit_bytes=VMEM_LIMIT),
    )(x2d, fox_o, sb_o, px, px, pg, wpool_bd, pscale, w_out)


def _constants(seq_len):
    bf16 = jnp.bfloat16
    idx = jnp.arange(MXU_TILE)
    hsum =(idx[:, None] // HEAD_DIM == idx[None, :] // HEAD_DIM).astype(bf16)
    c = jnp.arange(SCAN_CHUNK)
    scan_tri = (c[:, None] <= c[None, :]).astype(bf16)
    nc = seq_len // SCAN_CHUNK
    r = jnp.arange(FOX_HEADS * nc)
    scan_prev = ((r[:, None] // nc == r[None, :] // nc)
                 & (r[None, :] < r[:, None])).astype(bf16)
    a = jnp.arange(ATT_BK)
    sb_tri = (a[:, None] >= a[None, :]).astype(bf16)
    return hsum, scan_tri, scan_prev, sb_tri


def _block_diag(w_pool):
    out = jnp.zeros((w_pool.shape[0], POOL_W, POOL_W), w_pool.dtype)
    for g in range(POOL_GROUPS):
        lo = g * POOL_GROUP_DIM
        out = out.at[:, lo:lo + POOL_GROUP_DIM, lo:lo + POOL_GROUP_DIM].set(w_pool[:, g])
    return out.astype(jnp.bfloat16)


def kernel(x, norm_g, w_in, b_f, q_norm_g, k_norm_g, w_pool, pool_scale, w_out):
    b, s, d = x.shape
    depth = norm_g.shape[0]
    assert d == D_MODEL and s % ROW_BLOCK == 0
    assert s % ATT_BQ == 0 and ATT_BQ % ATT_BK == 0 and s % SB_BQ == 0 and SB_BQ % ATT_BK == 0
    assert w_in.shape[-1] == C_FF + FOX_HEADS
    hsum, scan_tri, scan_prev, sb_tri = _constants(s)
    nc = s // SCAN_CHUNK
    x2d = x.reshape(b * s, d)
    wpool_bd = _block_diag(w_pool)
    w_in_bf = w_in.astype(jnp.bfloat16)
    rowvec = lambda a: a[:, None, :]
    norm_g3, pscale3 = rowvec(norm_g), rowvec(pool_scale)
    b_f3 = rowvec(jnp.pad(b_f, ((0, 0), (0, FF_PAD - FOX_HEADS))))
    q_g = rowvec(jnp.tile(q_norm_g, (1, FOX_HEADS)))
    k_g = rowvec(jnp.tile(k_norm_g, (1, FOX_HEADS)))
    bounds = _fox_logit_bound(q_norm_g, k_norm_g)[:, None]
    for l in range(depth):
        (fq, fk, fv, fg, px, pg, sq, sk, sv, sg, lf) = _in_proj(
            l, x2d, norm_g3, w_in_bf, hsum, q_g, k_g, b_f3)

        c = _forget_scan(lf.reshape(FOX_HEADS, b * nc, SCAN_CHUNK), b, scan_tri, scan_prev)
        crow = c.reshape(b, FOX_HEADS // 2, 2, s)
        ccol = c.reshape(b, FOX_HEADS, s).transpose(0, 2, 1)

        r3 = lambda a: a.reshape(b, s, a.shape[-1])
        fox_o = _fox_attention(bounds[l:l + 1], r3(fq), r3(fk), r3(fv), ccol, crow, r3(fg))
        sb_o = _sb_attention(r3(sq), r3(sk), r3(sv), r3(sg), sb_tri)

        x2d = _out_proj(l, x2d, fox_o.reshape(b * s, FOX_W), sb_o.reshape(b * s, SB_W),
                        px, pg, wpool_bd, pscale3, w_out, s)
    return x2d.reshape(b, s, d)
```

```python
import functools

import jax
import jax.numpy as jnp
from jax import lax
from jax.experimental import pallas as pl
from jax.experimental.pallas import tpu as pltpu

D_MODEL = 1024
HEAD_DIM = 64
FOX_HEADS = 8
SB_HEADS = 4
POOL_GROUPS = 4
POOL_WINDOWS = (2, 4, 8, 16)
POOL_GROUP_DIM = 64
FOX_W = FOX_HEADS * HEAD_DIM
SB_W = SB_HEADS * HEAD_DIM
POOL_W = POOL_GROUPS * POOL_GROUP_DIM
D_MIX = FOX_W + POOL_W + SB_W
EPS = 1e-6
NEG = -1e30
SCALE = HEAD_DIM ** -0.5
LOG2E = 1.4426950408889634
FOX_MAX_BOUND = 30.0
SB_CUTOFF = 106.0
FOX_CUTOFF = 106.0

LANES = 128
MXU_TILE = 256
PAIR_W = 2 * HEAD_DIM
MAX_WINDOW = max(POOL_WINDOWS)
FF_PAD = LANES

ROW_BLOCK = 1024
WEIGHT_PACK_CHUNKS = 8
ATT_BQ = 1024
SB_BQ = 2048
ATT_BK = 256
FOX_BK_FULL = 512
FOX_BLOCKS_PER_TRIP = 2
SCAN_CHUNK = LANES

VMEM_LIMIT = 56 * 1024 * 1024

C_FQ, C_FK, C_FV, C_FG = 0, FOX_W, 2 * FOX_W, 3 * FOX_W
C_PX = 4 * FOX_W
C_PG = C_PX + POOL_W
C_SQ = C_PG + POOL_W
C_SK = C_SQ + SB_W
C_SV = C_SK + SB_W
C_SG = C_SV + SB_W
C_FF = C_SG + SB_W
D_IN_PAD = C_FF + FF_PAD


def _bf16_split3(x):
    a = x.astype(jnp.bfloat16)
    r = x - a.astype(jnp.float32)
    b = r.astype(jnp.bfloat16)
    c = (r - b.astype(jnp.float32)).astype(jnp.bfloat16)
    return a, b, c


def _dot(a, b):
    return jnp.dot(a, b, preferred_element_type=jnp.float32)


def _dot_nt(a, b):
    return lax.dot_general(a, b, (((1,), (1,)), ((), ())),
                           preferred_element_type=jnp.float32)


def _silu(x):
    return x * (1.0 / (1.0 + jnp.exp(-x)))


def _softplus(x):
    return jnp.maximum(x, 0.0) + jnp.log(1.0 + jnp.exp2(jnp.abs(x) * (-LOG2E)))


def _in_proj_kernel(x_ref, g_ref, win_ref, hsum_ref, qg_ref, kg_ref, bf_ref,
                    fq_ref, fk_ref, fv_ref, fg_ref, px_ref, pg_ref,
                    sq_ref, sk_ref, sv_ref, sg_ref, lf_ref, w_ref):
    @pl.when(pl.program_id(0) == 0)
    def _():
        chunk = D_MODEL // WEIGHT_PACK_CHUNKS
        for r in range(0, D_MODEL, chunk):
            rows = slice(r, r + chunk)
            w_ref[rows, 0:C_PX] = win_ref[rows, 0:C_PX].astype(jnp.bfloat16)
            w_ref[rows, C_PX:C_FF] = win_ref[rows, C_PX + FOX_HEADS:].astype(jnp.bfloat16)
            ff = win_ref[rows, C_PX:C_PX + FOX_HEADS].astype(jnp.bfloat16)
            w_ref[rows, C_FF:] = jnp.concatenate(
                [ff, jnp.zeros((chunk, FF_PAD - FOX_HEADS), jnp.bfloat16)], axis=1)

    x = x_ref[...]
    ms = jnp.mean(x * x, axis=-1, keepdims=True)
    h = (x * lax.rsqrt(ms + EPS) * g_ref[...]).astype(jnp.bfloat16)

    def proj(c0, width):
        return _dot(h, w_ref[:, c0:c0 + width])

    def head_norm(y, gain):
        sq = (y * y).astype(jnp.bfloat16)
        hsum = hsum_ref[...]
        w = hsum.shape[0]
        ssq = jnp.concatenate(
            [_dot(sq[:, c:c + w], hsum) for c in range(0, y.shape[1], w)], axis=1)
        return y * lax.rsqrt(ssq * (1.0 / HEAD_DIM) + EPS) * gain

    fq_ref[...] = (head_norm(proj(C_FQ, FOX_W), qg_ref[...]) * SCALE).astype(jnp.bfloat16)
    fk_ref[...] = head_norm(proj(C_FK, FOX_W), kg_ref[...]).astype(jnp.bfloat16)
    fv_ref[...] = proj(C_FV, FOX_W).astype(jnp.bfloat16)
    fg_ref[...] = proj(C_FG, FOX_W)
    px_ref[...] = proj(C_PX, POOL_W)
    pg_ref[...] = proj(C_PG, POOL_W)
    sq_ref[...] = (proj(C_SQ, SB_W) * SCALE).astype(jnp.bfloat16)
    sk_ref[...] = proj(C_SK, SB_W).astype(jnp.bfloat16)
    sv_ref[...] = proj(C_SV, SB_W).astype(jnp.bfloat16)
    sg_ref[...] = proj(C_SG, SB_W)
    ff = proj(C_FF, FF_PAD) + bf_ref[...]
    lf_ref[...] = (-_softplus(-ff)).T[:FOX_HEADS, :]


def _layer_spec(stacked, layer, single_buffer=False):
    tail = stacked.shape[1:]
    mode = dict(pipeline_mode=pl.Buffered(1)) if single_buffer else {}
    return pl.BlockSpec((None,) + tail, lambda i: (layer,) + (0,) * len(tail), **mode)


def _in_proj(layer, x2d, norm_g, w_in, hsum, q_g, k_g, b_f):
    m = x2d.shape[0]
    bm = ROW_BLOCK
    row = lambda w: pl.BlockSpec((bm, w), lambda i: (i, 0))
    full = lambda a: pl.BlockSpec(a.shape, lambda i: (0,) * a.ndim)
    per_layer = lambda a: _layer_spec(a, layer)
    bf16, f32 = jnp.bfloat16, jnp.float32
    outs = [(FOX_W, bf16), (FOX_W, bf16), (FOX_W, bf16), (FOX_W, f32),
            (POOL_W, f32), (POOL_W, f32),
            (SB_W, bf16), (SB_W, bf16), (SB_W, bf16), (SB_W, f32)]
    return pl.pallas_call(
        _in_proj_kernel,
        grid=(m // bm,),
        in_specs=[row(D_MODEL), per_layer(norm_g), _layer_spec(w_in, layer, single_buffer=True),
                  full(hsum), per_layer(q_g), per_layer(k_g), per_layer(b_f)],
        out_specs=[row(w) for w, _ in outs] + [pl.BlockSpec((FOX_HEADS, bm), lambda i: (0, i))],
        out_shape=([jax.ShapeDtypeStruct((m, w), dt) for w, dt in outs]
                   + [jax.ShapeDtypeStruct((FOX_HEADS, m), f32)]),
        scratch_shapes=[pltpu.VMEM((D_MODEL, D_IN_PAD), jnp.bfloat16)],
        compiler_params=pltpu.CompilerParams(
            dimension_semantics=("arbitrary",), vmem_limit_bytes=VMEM_LIMIT),
    )(x2d, norm_g, w_in, hsum, q_g, k_g, b_f)


def _scan_kernel(lf_ref, tri_ref, prev_ref, c_ref):
    lf = lf_ref[...].reshape(c_ref.shape[1:])
    tri = tri_ref[...]
    within = sum(_dot(p, tri) for p in _bf16_split3(lf))
    total = jnp.broadcast_to(within[:, SCAN_CHUNK - 1:SCAN_CHUNK], within.shape)
    prev = prev_ref[...]
    offset = sum(_dot(prev, p) for p in _bf16_split3(total))
    c_ref[0] = within + offset


def _forget_scan(lf_t, b, tri, prev):
    heads, rows, c = lf_t.shape
    nc = rows // b
    r = heads * nc
    full = lambda a: pl.BlockSpec(a.shape, lambda i: (0,) * a.ndim)
    blk = pl.BlockSpec((1, r, c), lambda i: (i, 0, 0))
    return pl.pallas_call(
        _scan_kernel,
        grid=(b,),
        in_specs=[pl.BlockSpec((heads, nc, c), lambda i: (0, i, 0)), full(tri), full(prev)],
        out_specs=blk,
        out_shape=jax.ShapeDtypeStruct((b, r, c), jnp.float32),
        compiler_params=pltpu.CompilerParams(dimension_semantics=("arbitrary",)),
    )(lf_t, tri, prev)


def _lane_is_first_head(shape):
    return lax.broadcasted_iota(jnp.int32, shape, len(shape) - 1) < HEAD_DIM


def _split_heads(q):
    first = _lane_is_first_head(q.shape)
    zero = jnp.zeros_like(q)
    return jnp.where(first, q, zero), jnp.where(first, zero, q)


def _widen(col, width):
    return jnp.concatenate([col] * (width // LANES), axis=1)


def _fox_kernel(bound_ref, *refs):
    bounded = bound_ref[0, 0] <= FOX_MAX_BOUND

    @pl.when(bounded)
    def _():
        _fox_path(bound_ref, *refs, running_max=False)

    @pl.when(jnp.logical_not(bounded))
    def _():
        _fox_path(bound_ref, *refs, running_max=True)


def _fox_path(bound_ref, q_ref, k_ref, v_ref, ccol_ref, crow_ref, g_ref, o_ref,
              qs_ref, m_ref, acc_ref, ct_ref, *, running_max):
    bq, bk = ATT_BQ, ATT_BK
    qi = pl.program_id(2)
    q0, q1 = _split_heads(q_ref[0])
    qs_ref[0] = q0
    qs_ref[1] = q1
    shift = 0.0 if running_max else bound_ref[0, 0]
    c_all = ccol_ref[0]
    head_lane = lax.broadcasted_iota(jnp.int32, c_all.shape, 1)
    c_first = []
    for h in range(2):
        m_ref[h] = jnp.full((bq, LANES), NEG, jnp.float32)
        acc_ref[h] = jnp.zeros((bq, PAIR_W), jnp.float32)
        mine = head_lane == 2 * pl.program_id(1) + h
        ct = jnp.sum(jnp.where(mine, c_all, 0.0), axis=1, keepdims=True)
        c_first.append(ct[0:1, :])
        ct_ref[h] = jnp.broadcast_to(ct, (bq, LANES)) - shift

    def run(blocks, masked, heads=(0, 1)):
        chains = []
        for j, bk, r0 in blocks:
            n = bq - r0
            start = pl.multiple_of(j * bk, bk)
            k = k_ref[0, pl.ds(start, bk), :]
            v = v_ref[0, pl.ds(start, bk), :]
            first = _lane_is_first_head(v.shape)
            ones = jnp.ones_like(v)
            vs = (jnp.where(first, v, ones), jnp.where(first, ones, v))
            keep = None
            if masked:
                keep = (lax.broadcasted_iota(jnp.int32, (n, bk), 1)
                        <= lax.broadcasted_iota(jnp.int32, (n, bk), 0))
            for h in heads:
                cs = crow_ref[0, 0, h:h + 1, pl.ds(start, bk)]
                chains.append((h, slice(r0, bq), bk, k, vs[h], cs, keep))
        qk = [_dot_nt(qs_ref[h, rows, :], k) for h, rows, _, k, _, _, _ in chains]
        ps = []
        for (h, rows, bk, _, _, cs, keep), s in zip(chains, qk):
            s = s + (_widen(ct_ref[h, rows, :], bk) - cs)
            if masked:
                s = jnp.where(keep, s, NEG)
            if running_max:
                m_prev = m_ref[h, rows, :]
                m_next = jnp.maximum(m_prev, jnp.max(s, axis=1, keepdims=True))
                s = s - _widen(m_next, bk)
                acc_ref[h, rows, :] = jnp.exp(m_prev - m_next) * acc_ref[h, rows, :]
                m_ref[h, rows, :] = m_next
            ps.append(jnp.exp(s).astype(jnp.bfloat16))
        for (h, rows, _, _, v, _, _), p in zip(chains, ps):
            acc_ref[h, rows, :] = acc_ref[h, rows, :] + _dot(p, v)

    sub = bq // bk
    diagonal = [(qi * sub + jj, bk, jj * bk) for jj in reversed(range(sub))]
    if running_max:
        for blk in diagonal:
            run([blk], True)
    else:
        run(diagonal, True)

    n_blocks = qi * (bq // FOX_BK_FULL)

    def span(first, count, n, heads):
        def body(j, carry):
            run([(first + j * n + u, FOX_BK_FULL, 0) for u in range(n)], False, heads)
            return carry
        lax.fori_loop(0, count, body, 0)

    if running_max:
        span(0, n_blocks, 1, (0, 1))
    else:
        pos = lax.broadcasted_iota(jnp.int32, (1, crow_ref.shape[-1]), 1)
        live_from = []
        for h in range(2):
            bias = c_first[h] - crow_ref[0, 0, h:h + 1, :]
            dead = jnp.logical_and(bias <= -FOX_CUTOFF, pos < qi * bq)
            n_dead = jnp.sum(jnp.where(dead, 1.0, 0.0)).astype(jnp.int32)
            live_from.append(n_dead // FOX_BK_FULL)
        both_from = jnp.maximum(live_from[0], live_from[1])
        pair = FOX_BLOCKS_PER_TRIP
        for h in range(2):
            alone = both_from - live_from[h]
            span(live_from[h], alone % pair, 1, (h,))
            span(live_from[h] + alone % pair, alone // pair, pair, (h,))
        shared = n_blocks - both_from
        span(both_from, shared % pair, 1, (0, 1))
        start = both_from + shared % pair
        span(start, (shared // pair) % 2, pair, (0, 1))
        span(start + ((shared // pair) % 2) * pair, shared // (2 * pair), 2 * pair, (0, 1))

    a0, a1 = acc_ref[0], acc_ref[1]
    first = _lane_is_first_head(a0.shape)
    values = jnp.where(first, a0, a1)
    sums = pltpu.roll(jnp.where(first, a1, a0), HEAD_DIM, axis=1)
    o_ref[0] = (values / sums * _silu(g_ref[0])).astype(o_ref.dtype)


def _fox_attention(bound, q, k, v, ccol, crow, gate):
    b, s, w = q.shape
    t = ATT_BQ
    pairs = w // PAIR_W
    qblk = pl.BlockSpec((1, t, PAIR_W), lambda bi, hp, qi: (bi, qi, hp))
    seq = pl.BlockSpec((1, s, PAIR_W), lambda bi, hp, qi: (bi, 0, hp))
    return pl.pallas_call(
        _fox_kernel,
        grid=(b, pairs, s // t),
        in_specs=[pl.BlockSpec(memory_space=pltpu.SMEM), qblk, seq, seq,
                  pl.BlockSpec((1, t, ccol.shape[-1]), lambda bi, hp, qi: (bi, qi, 0)),
                  pl.BlockSpec((1, 1, 2, s), lambda bi, hp, qi: (bi, hp, 0, 0)),
                  qblk],
        out_specs=qblk,
        out_shape=jax.ShapeDtypeStruct((b, s, w), jnp.bfloat16),
        scratch_shapes=[pltpu.VMEM((2, t, PAIR_W), jnp.bfloat16),
                        pltpu.VMEM((2, t, LANES), jnp.float32),
                        pltpu.VMEM((2, t, PAIR_W), jnp.float32),
                        pltpu.VMEM((2, t, LANES), jnp.float32)],
        compiler_params=pltpu.CompilerParams(
            dimension_semantics=("arbitrary", "arbitrary", "arbitrary"),
            vmem_limit_bytes=VMEM_LIMIT),
    )(bound, q, k, v, ccol, crow, gate)


def _fox_logit_bound(q_gain, k_gain):
    gq = jnp.max(jnp.abs(q_gain), axis=-1)
    gk = jnp.max(jnp.abs(k_gain), axis=-1)
    return (HEAD_DIM * SCALE * 1.01) * gq * gk + 0.05


def _sb_kernel(q_ref, k_ref, v_ref, g_ref, tri_ref, o_ref, qs_ref, r_ref, acc_ref):
    bq, t = SB_BQ, ATT_BK
    sub = bq // t
    qi = pl.program_id(2)
    q0, q1 = _split_heads(q_ref[0])
    qs_ref[0] = q0
    qs_ref[1] = q1
    for h in range(2):
        r_ref[h] = jnp.zeros((bq, LANES), jnp.float32)
        acc_ref[h] = jnp.zeros((bq, PAIR_W), jnp.float32)

    def step(distances):
        tri = tri_ref[...]
        keep = (lax.broadcasted_iota(jnp.int32, (t, t), 1)
                < lax.broadcasted_iota(jnp.int32, (t, t), 0))
        rows = {a: slice(a * t, (a + 1) * t) for a in range(sub)}
        chains, valid, ks, vs = [], {}, {}, {}
        for n, d in enumerate(distances):
            for a in range(sub):
                j = qi * sub + a - d
                valid[n, a] = j >= 0
                start = pl.multiple_of(jnp.maximum(j, 0) * t, t)
                ks[n, a] = k_ref[0, pl.ds(start, t), :]
                vs[n, a] = v_ref[0, pl.ds(start, t), :]
                chains += [(n, a, h) for h in range(2)]
        diagonal = {n: isinstance(d, int) and d == 0 for n, d in enumerate(distances)}
        z = {(n, a, h): _dot_nt(qs_ref[h, rows[a], :], ks[n, a]) for n, a, h in chains}
        sps = {}
        for c in chains:
            sp = _softplus(z[c])
            if diagonal[c[0]]:
                sp = jnp.where(keep, sp, 0.0)
            sps[c] = sp.astype(jnp.bfloat16)
        cum = {c: _dot(sps[c], tri) for c in chains}
        w = {}
        for c in chains:
            wc = jnp.exp(z[c] - cum[c])
            if diagonal[c[0]]:
                wc = jnp.where(keep, wc, 0.0)
            w[c] = wc.astype(jnp.bfloat16)
        pv = {(n, a, h): _dot(w[n, a, h], vs[n, a]) for n, a, h in chains}
        for n, a, h in chains:
            r_prev = r_ref[h, rows[a], :]
            add = jnp.exp(-r_prev) * pv[n, a, h]
            mass = jnp.broadcast_to(cum[n, a, h][:, 0:1], (t, LANES))
            if not diagonal[n]:
                add = jnp.where(valid[n, a], add, 0.0)
                mass = jnp.where(valid[n, a], mass, 0.0)
            acc_ref[h, rows[a], :] = acc_ref[h, rows[a], :] + add
            r_ref[h, rows[a], :] = r_prev + mass

    def finished(d):
        done = jnp.bool_(True)
        for a in range(sub):
            rows = slice(a * t, (a + 1) * t)
            exhausted = qi * sub + a - (d + 1) < 0
            saturated = jnp.min(r_ref[:, rows, :]) >= SB_CUTOFF
            done = jnp.logical_and(done, jnp.logical_or(exhausted, saturated))
        return done

    step([0, 1])

    def cond(carry):
        _, done = carry
        return jnp.logical_not(done)

    def body(carry):
        d, _ = carry
        step([d])
        return d + 1, finished(d)

    lax.while_loop(cond, body, (jnp.int32(2), finished(1)))

    o = jnp.where(_lane_is_first_head((bq, PAIR_W)), acc_ref[0], acc_ref[1])
    o_ref[0] = (o * _silu(g_ref[0])).astype(o_ref.dtype)


def _sb_attention(q, k, v, gate, tri):
    b, s, w = q.shape
    t = SB_BQ
    pairs = w // PAIR_W
    qblk = pl.BlockSpec((1, t, PAIR_W), lambda bi, hp, qi: (bi, qi, hp))
    seq = pl.BlockSpec((1, s, PAIR_W), lambda bi, hp, qi: (bi, 0, hp))
    return pl.pallas_call(
        _sb_kernel,
        grid=(b, pairs, s // t),
        in_specs=[qblk, seq, seq, qblk,
                  pl.BlockSpec(tri.shape, lambda bi, hp, qi: (0, 0))],
        out_specs=qblk,
        out_shape=jax.ShapeDtypeStruct((b, s, w), jnp.bfloat16),
        scratch_shapes=[pltpu.VMEM((2, t, PAIR_W), jnp.bfloat16),
                        pltpu.VMEM((2, t, LANES), jnp.float32),
                        pltpu.VMEM((2, t, PAIR_W), jnp.float32)],
        compiler_params=pltpu.CompilerParams(
            dimension_semantics=("arbitrary", "arbitrary", "arbitrary"),
            vmem_limit_bytes=VMEM_LIMIT),
    )(q, k, v, gate, tri)


def _out_proj_kernel(x_ref, fox_ref, sb_ref, px_ref, hist_ref, pg_ref,
                     wpool_ref, pscale_ref, wout_ref, o_ref, xp_ref, wo_ref, *, blocks_per_seq):
    bm = ROW_BLOCK
    i = pl.program_id(0)

    @pl.when(i == 0)
    def _():
        chunk = D_MIX // WEIGHT_PACK_CHUNKS
        for r in range(0, D_MIX, chunk):
            wo_ref[r:r + chunk, :] = wout_ref[r:r + chunk, :].astype(jnp.bfloat16)

    pos0 = (i % blocks_per_seq) * bm
    hist = hist_ref[...]
    xp_ref[0:MAX_WINDOW, :] = jnp.where(pos0 == 0, jnp.zeros_like(hist), hist)
    px = px_ref[...]
    xp_ref[MAX_WINDOW:MAX_WINDOW + bm, :] = px

    sums = {}
    run = px
    for d in range(1, MAX_WINDOW):
        run = run + xp_ref[MAX_WINDOW - d:MAX_WINDOW - d + bm, :]
        if d + 1 in POOL_WINDOWS:
            sums[d + 1] = run
    group = lax.broadcasted_iota(jnp.int32, (bm, POOL_W), 1) // POOL_GROUP_DIM
    pos = pos0 + lax.broadcasted_iota(jnp.int32, (bm, POOL_W), 0)
    wsum = sums[POOL_WINDOWS[-1]]
    win = jnp.full((bm, POOL_W), POOL_WINDOWS[-1], jnp.int32)
    for g in range(POOL_GROUPS - 2, -1, -1):
        wsum = jnp.where(group == g, sums[POOL_WINDOWS[g]], wsum)
        win = jnp.where(group == g, POOL_WINDOWS[g], win)
    count = jnp.minimum(pos + 1, win).astype(jnp.float32)
    pooled = wsum / count - px
    y = _dot(pooled.astype(jnp.bfloat16), wpool_ref[...]) * pscale_ref[...]
    pool_out = (y * _silu(pg_ref[...])).astype(jnp.bfloat16)

    o_ref[...] = (x_ref[...]
                  + _dot(fox_ref[...], wo_ref[0:FOX_W, :])
                  + _dot(pool_out, wo_ref[FOX_W:FOX_W + POOL_W, :])
                  + _dot(sb_ref[...], wo_ref[FOX_W + POOL_W:D_MIX, :]))


def _out_proj(layer, x2d, fox_o, sb_o, px, pg, wpool_bd, pscale, w_out, seq_len):
    m = x2d.shape[0]
    bm = ROW_BLOCK
    row = lambda w: pl.BlockSpec((bm, w), lambda i: (i, 0))
    per_layer = lambda a: _layer_spec(a, layer)
    hist_per_block = bm // MAX_WINDOW
    hist = pl.BlockSpec((MAX_WINDOW, POOL_W),
                        lambda i: (jnp.maximum(i * hist_per_block - 1, 0), 0))
    kern = functools.partial(_out_proj_kernel, blocks_per_seq=seq_len // bm)
    return pl.pallas_call(
        kern,
        grid=(m // bm,),
        in_specs=[row(D_MODEL), row(FOX_W), row(SB_W), row(POOL_W), hist, row(POOL_W),
                  per_layer(wpool_bd), per_layer(pscale),
                  _layer_spec(w_out, layer, single_buffer=True)],
        out_specs=row(D_MODEL),
        out_shape=jax.ShapeDtypeStruct((m, D_MODEL), jnp.float32),
        scratch_shapes=[pltpu.VMEM((MAX_WINDOW + bm, POOL_W), jnp.float32),
                        pltpu.VMEM((D_MIX, D_MODEL), jnp.bfloat16)],
        compiler_params=pltpu.CompilerParams(
            dimension_semantics=("arbitrary",), vmem_limit_bytes=VMEM_LIMIT),
    )(x2d, fox_o, sb_o, px, px, pg, wpool_bd, pscale, w_out)


def _constants(seq_len):
    bf16 = jnp.bfloat16
    idx = jnp.arange(MXU_TILE)
    hsum =(idx[:, None] // HEAD_DIM == idx[None, :] // HEAD_DIM).astype(bf16)
    c = jnp.arange(SCAN_CHUNK)
    scan_tri = (c[:, None] <= c[None, :]).astype(bf16)
    nc = seq_len // SCAN_CHUNK
    r = jnp.arange(FOX_HEADS * nc)
    scan_prev = ((r[:, None] // nc == r[None, :] // nc)
                 & (r[None, :] < r[:, None])).astype(bf16)
    a = jnp.arange(ATT_BK)
    sb_tri = (a[:, None] >= a[None, :]).astype(bf16)
    return hsum, scan_tri, scan_prev, sb_tri


def _block_diag(w_pool):
    out = jnp.zeros((w_pool.shape[0], POOL_W, POOL_W), w_pool.dtype)
    for g in range(POOL_GROUPS):
        lo = g * POOL_GROUP_DIM
        out = out.at[:, lo:lo + POOL_GROUP_DIM, lo:lo + POOL_GROUP_DIM].set(w_pool[:, g])
    return out.astype(jnp.bfloat16)


def kernel(x, norm_g, w_in, b_f, q_norm_g, k_norm_g, w_pool, pool_scale, w_out):
    b, s, d = x.shape
    depth = norm_g.shape[0]
    assert d == D_MODEL and s % ROW_BLOCK == 0
    assert s % ATT_BQ == 0 and ATT_BQ % ATT_BK == 0 and s % SB_BQ == 0 and SB_BQ % ATT_BK == 0
    assert w_in.shape[-1] == C_FF + FOX_HEADS
    hsum, scan_tri, scan_prev, sb_tri = _constants(s)
    nc = s // SCAN_CHUNK
    x2d = x.reshape(b * s, d)
    wpool_bd = _block_diag(w_pool)
    w_in_bf = w_in.astype(jnp.bfloat16)
    rowvec = lambda a: a[:, None, :]
    norm_g3, pscale3 = rowvec(norm_g), rowvec(pool_scale)
    b_f3 = rowvec(jnp.pad(b_f, ((0, 0), (0, FF_PAD - FOX_HEADS))))
    q_g = rowvec(jnp.tile(q_norm_g, (1, FOX_HEADS)))
    k_g = rowvec(jnp.tile(k_norm_g, (1, FOX_HEADS)))
    bounds = _fox_logit_bound(q_norm_g, k_norm_g)[:, None]
    for l in range(depth):
        (fq, fk, fv, fg, px, pg, sq, sk, sv, sg, lf) = _in_proj(
            l, x2d, norm_g3, w_in_bf, hsum, q_g, k_g, b_f3)

        c = _forget_scan(lf.reshape(FOX_HEADS, b * nc, SCAN_CHUNK), b, scan_tri, scan_prev)
        crow = c.reshape(b, FOX_HEADS // 2, 2, s)
        ccol = c.reshape(b, FOX_HEADS, s).transpose(0, 2, 1)

        r3 = lambda a: a.reshape(b, s, a.shape[-1])
        fox_o = _fox_attention(bounds[l:l + 1], r3(fq), r3(fk), r3(fv), ccol, crow, r3(fg))
        sb_o = _sb_attention(r3(sq), r3(sk), r3(sv), r3(sg), sb_tri)

        x2d = _out_proj(l, x2d, fox_o.reshape(b * s, FOX_W), sb_o.reshape(b * s, SB_W),
                        px, pg, wpool_bd, pscale3, w_out, s)
    return x2d.reshape(b, s, d)
```

```python
import functools

import jax
import jax.numpy as jnp
from jax import lax
from jax.experimental import pallas as pl
from jax.experimental.pallas import tpu as pltpu

D_MODEL = 1024
HEAD_DIM = 64
FOX_HEADS = 8
SB_HEADS = 4
POOL_GROUPS = 4
POOL_WINDOWS = (2, 4, 8, 16)
POOL_GROUP_DIM = 64
FOX_W = FOX_HEADS * HEAD_DIM
SB_W = SB_HEADS * HEAD_DIM
POOL_W = POOL_GROUPS * POOL_GROUP_DIM
D_MIX = FOX_W + POOL_W + SB_W
EPS = 1e-6
NEG = -1e30
SCALE = HEAD_DIM ** -0.5
LOG2E = 1.4426950408889634
FOX_MAX_BOUND = 30.0
SB_CUTOFF = 106.0
FOX_CUTOFF = 106.0

LANES = 128
MXU_TILE = 256
PAIR_W = 2 * HEAD_DIM
MAX_WINDOW = max(POOL_WINDOWS)
FF_PAD = LANES

ROW_BLOCK = 1024
WEIGHT_PACK_CHUNKS = 8
ATT_BQ = 1024
SB_BQ = 2048
ATT_BK = 256
FOX_BK_FULL = 512
FOX_BLOCKS_PER_TRIP = 2
SCAN_CHUNK = LANES

VMEM_LIMIT = 56 * 1024 * 1024

C_FQ, C_FK, C_FV, C_FG = 0, FOX_W, 2 * FOX_W, 3 * FOX_W
C_PX = 4 * FOX_W
C_PG = C_PX + POOL_W
C_SQ = C_PG + POOL_W
C_SK = C_SQ + SB_W
C_SV = C_SK + SB_W
C_SG = C_SV + SB_W
C_FF = C_SG + SB_W
D_IN_PAD = C_FF + FF_PAD


def _bf16_split3(x):
    a = x.astype(jnp.bfloat16)
    r = x - a.astype(jnp.float32)
    b = r.astype(jnp.bfloat16)
    c = (r - b.astype(jnp.float32)).astype(jnp.bfloat16)
    return a, b, c


def _dot(a, b):
    return jnp.dot(a, b, preferred_element_type=jnp.float32)


def _dot_nt(a, b):
    return lax.dot_general(a, b, (((1,), (1,)), ((), ())),
                           preferred_element_type=jnp.float32)


def _silu(x):
    return x * (1.0 / (1.0 + jnp.exp(-x)))


def _softplus(x):
    return jnp.maximum(x, 0.0) + jnp.log(1.0 + jnp.exp2(jnp.abs(x) * (-LOG2E)))


def _in_proj_kernel(x_ref, g_ref, win_ref, hsum_ref, qg_ref, kg_ref, bf_ref,
                    fq_ref, fk_ref, fv_ref, fg_ref, px_ref, pg_ref,
                    sq_ref, sk_ref, sv_ref, sg_ref, lf_ref, w_ref):
    @pl.when(pl.program_id(0) == 0)
    def _():
        chunk = D_MODEL // WEIGHT_PACK_CHUNKS
        for r in range(0, D_MODEL, chunk):
            rows = slice(r, r + chunk)
            w_ref[rows, 0:C_PX] = win_ref[rows, 0:C_PX].astype(jnp.bfloat16)
            w_ref[rows, C_PX:C_FF] = win_ref[rows, C_PX + FOX_HEADS:].astype(jnp.bfloat16)
            ff = win_ref[rows, C_PX:C_PX + FOX_HEADS].astype(jnp.bfloat16)
            w_ref[rows, C_FF:] = jnp.concatenate(
                [ff, jnp.zeros((chunk, FF_PAD - FOX_HEADS), jnp.bfloat16)], axis=1)

    x = x_ref[...]
    ms = jnp.mean(x * x, axis=-1, keepdims=True)
    h = (x * lax.rsqrt(ms + EPS) * g_ref[...]).astype(jnp.bfloat16)

    def proj(c0, width):
        return _dot(h, w_ref[:, c0:c0 + width])

    def head_norm(y, gain):
        sq = (y * y).astype(jnp.bfloat16)
        hsum = hsum_ref[...]
        w = hsum.shape[0]
        ssq = jnp.concatenate(
            [_dot(sq[:, c:c + w], hsum) for c in range(0, y.shape[1], w)], axis=1)
        return y * lax.rsqrt(ssq * (1.0 / HEAD_DIM) + EPS) * gain

    fq_ref[...] = (head_norm(proj(C_FQ, FOX_W), qg_ref[...]) * SCALE).astype(jnp.bfloat16)
    fk_ref[...] = head_norm(proj(C_FK, FOX_W), kg_ref[...]).astype(jnp.bfloat16)
    fv_ref[...] = proj(C_FV, FOX_W).astype(jnp.bfloat16)
    fg_ref[...] = proj(C_FG, FOX_W)
    px_ref[...] = proj(C_PX, POOL_W)
    pg_ref[...] = proj(C_PG, POOL_W)
    sq_ref[...] = (proj(C_SQ, SB_W) * SCALE).astype(jnp.bfloat16)
    sk_ref[...] = proj(C_SK, SB_W).astype(jnp.bfloat16)
    sv_ref[...] = proj(C_SV, SB_W).astype(jnp.bfloat16)
    sg_ref[...] = proj(C_SG, SB_W)
    ff = proj(C_FF, FF_PAD) + bf_ref[...]
    lf_ref[...] = (-_softplus(-ff)).T[:FOX_HEADS, :]


def _layer_spec(stacked, layer, single_buffer=False):
    tail = stacked.shape[1:]
    mode = dict(pipeline_mode=pl.Buffered(1)) if single_buffer else {}
    return pl.BlockSpec((None,) + tail, lambda i: (layer,) + (0,) * len(tail), **mode)


def _in_proj(layer, x2d, norm_g, w_in, hsum, q_g, k_g, b_f):
    m = x2d.shape[0]
    bm = ROW_BLOCK
    row = lambda w: pl.BlockSpec((bm, w), lambda i: (i, 0))
    full = lambda a: pl.BlockSpec(a.shape, lambda i: (0,) * a.ndim)
    per_layer = lambda a: _layer_spec(a, layer)
    bf16, f32 = jnp.bfloat16, jnp.float32
    outs = [(FOX_W, bf16), (FOX_W, bf16), (FOX_W, bf16), (FOX_W, f32),
            (POOL_W, f32), (POOL_W, f32),
            (SB_W, bf16), (SB_W, bf16), (SB_W, bf16), (SB_W, f32)]
    return pl.pallas_call(
        _in_proj_kernel,
        grid=(m // bm,),
        in_specs=[row(D_MODEL), per_layer(norm_g), _layer_spec(w_in, layer, single_buffer=True),
                  full(hsum), per_layer(q_g), per_layer(k_g), per_layer(b_f)],
        out_specs=[row(w) for w, _ in outs] + [pl.BlockSpec((FOX_HEADS, bm), lambda i: (0, i))],
        out_shape=([jax.ShapeDtypeStruct((m, w), dt) for w, dt in outs]
                   + [jax.ShapeDtypeStruct((FOX_HEADS, m), f32)]),
        scratch_shapes=[pltpu.VMEM((D_MODEL, D_IN_PAD), jnp.bfloat16)],
        compiler_params=pltpu.CompilerParams(
            dimension_semantics=("arbitrary",), vmem_limit_bytes=VMEM_LIMIT),
    )(x2d, norm_g, w_in, hsum, q_g, k_g, b_f)


def _lane_is_first_head(shape):
    return lax.broadcasted_iota(jnp.int32, shape, len(shape) - 1) < HEAD_DIM


def _split_heads(q):
    first = _lane_is_first_head(q.shape)
    zero = jnp.zeros_like(q)
    return jnp.where(first, q, zero), jnp.where(first, zero, q)


def _widen(col, width):
    return jnp.concatenate([col] * (width // LANES), axis=1)


def _fox_kernel(bound_ref, lf_ref, tri_ref, prev_ref, *refs):
    c_ref = refs[-1]

    @pl.when(pl.program_id(2) == 0)
    def _():
        tri, prev = tri_ref[...], prev_ref[...]
        for h in range(2):
            lf = lf_ref[2 * pl.program_id(1) + h]
            within = sum(_dot(p, tri) for p in _bf16_split3(lf))
            total = jnp.broadcast_to(within[:, SCAN_CHUNK - 1:SCAN_CHUNK], within.shape)
            c_ref[h] = within + sum(_dot(prev, p) for p in _bf16_split3(total))

    bounded = bound_ref[0, 0] <= FOX_MAX_BOUND

    @pl.when(bounded)
    def _():
        _fox_path(bound_ref, *refs, running_max=False)

    @pl.when(jnp.logical_not(bounded))
    def _():
        _fox_path(bound_ref, *refs, running_max=True)


def _fox_path(bound_ref, q_ref, k_ref, v_ref, g_ref, o_ref,
              qs_ref, m_ref, acc_ref, ct_ref, c_ref, *, running_max):
    bq, bk = ATT_BQ, ATT_BK
    qi = pl.program_id(2)
    q0, q1 = _split_heads(q_ref[0])
    qs_ref[0] = q0
    qs_ref[1] = q1
    shift = 0.0 if running_max else bound_ref[0, 0]
    chunks = bq // SCAN_CHUNK
    c_first = []
    for h in range(2):
        m_ref[h] = jnp.full((bq, LANES), NEG, jnp.float32)
        acc_ref[h] = jnp.zeros((bq, PAIR_W), jnp.float32)
        mine = c_ref[h, pl.ds(pl.multiple_of(qi * chunks, chunks), chunks), :]
        cols = jnp.tile(mine, (LANES // chunks, 1)).T
        c_first.append(cols[0:1, 0:1])
        for n in range(chunks):
            ct_ref[h, n * SCAN_CHUNK:(n + 1) * SCAN_CHUNK, :] = (
                jnp.broadcast_to(cols[:, n:n + 1], (SCAN_CHUNK, LANES)) - shift)

    def run(blocks, masked, heads=(0, 1)):
        chains = []
        for j, bk, r0 in blocks:
            n = bq - r0
            start = pl.multiple_of(j * bk, bk)
            k = k_ref[0, pl.ds(start, bk), :]
            v = v_ref[0, pl.ds(start, bk), :]
            first = _lane_is_first_head(v.shape)
            ones = jnp.ones_like(v)
            vs = (jnp.where(first, v, ones), jnp.where(first, ones, v))
            keep = None
            if masked:
                keep = (lax.broadcasted_iota(jnp.int32, (n, bk), 1)
                        <= lax.broadcasted_iota(jnp.int32, (n, bk), 0))
            for h in heads:
                first_chunk = pl.multiple_of(j * (bk // SCAN_CHUNK), bk // SCAN_CHUNK)
                cs = jnp.concatenate([c_ref[h, pl.ds(first_chunk + n, 1), :]
                                      for n in range(bk // SCAN_CHUNK)], axis=1)
                chains.append((h, slice(r0, bq), bk, k, vs[h], cs, keep))
        qk = [_dot_nt(qs_ref[h, rows, :], k) for h, rows, _, k, _, _, _ in chains]
        ps = []
        for (h, rows, bk, _, _, cs, keep), s in zip(chains, qk):
            s = s + (_widen(ct_ref[h, rows, :], bk) - cs)
            if masked:
                s = jnp.where(keep, s, NEG)
            if running_max:
                m_prev = m_ref[h, rows, :]
                m_next = jnp.maximum(m_prev, jnp.max(s, axis=1, keepdims=True))
                s = s - _widen(m_next, bk)
                acc_ref[h, rows, :] = jnp.exp(m_prev - m_next) * acc_ref[h, rows, :]
                m_ref[h, rows, :] = m_next
            ps.append(jnp.exp(s).astype(jnp.bfloat16))
        for (h, rows, _, _, v, _, _), p in zip(chains, ps):
            acc_ref[h, rows, :] = acc_ref[h, rows, :] + _dot(p, v)

    sub = bq // bk
    diagonal = [(qi * sub + jj, bk, jj * bk) for jj in reversed(range(sub))]
    if running_max:
        for blk in diagonal:
            run([blk], True)
    else:
        run(diagonal, True)

    n_blocks = qi * (bq // FOX_BK_FULL)

    def span(first, count, n, heads):
        def body(j, carry):
            run([(first + j * n + u, FOX_BK_FULL, 0) for u in range(n)], False, heads)
            return carry
        lax.fori_loop(0, count, body, 0)

    if running_max:
        span(0, n_blocks, 1, (0, 1))
    else:
        pos = (lax.broadcasted_iota(jnp.int32, c_ref.shape[1:], 0) * SCAN_CHUNK
               + lax.broadcasted_iota(jnp.int32, c_ref.shape[1:], 1))
        live_from = []
        for h in range(2):
            bias = c_first[h] - c_ref[h]
            dead = jnp.logical_and(bias <= -FOX_CUTOFF, pos < qi * bq)
            n_dead = jnp.sum(jnp.where(dead, 1.0, 0.0)).astype(jnp.int32)
            live_from.append(n_dead // FOX_BK_FULL)
        both_from = jnp.maximum(live_from[0], live_from[1])
        pair = FOX_BLOCKS_PER_TRIP
        for h in range(2):
            alone = both_from - live_from[h]
            span(live_from[h], alone % pair, 1, (h,))
            span(live_from[h] + alone % pair, alone // pair, pair, (h,))
        shared = n_blocks - both_from
        span(both_from, shared % pair, 1, (0, 1))
        start = both_from + shared % pair
        span(start, (shared // pair) % 2, pair, (0, 1))
        span(start + ((shared // pair) % 2) * pair, shared // (2 * pair), 2 * pair, (0, 1))

    a0, a1 = acc_ref[0], acc_ref[1]
    first = _lane_is_first_head(a0.shape)
    values = jnp.where(first, a0, a1)
    sums = pltpu.roll(jnp.where(first, a1, a0), HEAD_DIM, axis=1)
    o_ref[0] = (values / sums * _silu(g_ref[0])).astype(o_ref.dtype)


def _fox_attention(bound, lf_t, tri, prev, q, k, v, gate):
    b, s, w = q.shape
    t = ATT_BQ
    pairs = w // PAIR_W
    nc = s // SCAN_CHUNK
    qblk = pl.BlockSpec((1, t, PAIR_W), lambda bi, hp, qi: (bi, qi, hp))
    seq = pl.BlockSpec((1, s, PAIR_W), lambda bi, hp, qi: (bi, 0, hp))
    return pl.pallas_call(
        _fox_kernel,
        grid=(b, pairs, s // t),
        in_specs=[pl.BlockSpec(memory_space=pltpu.SMEM),
                  pl.BlockSpec((lf_t.shape[0], nc, SCAN_CHUNK), lambda bi, hp, qi: (0, bi, 0)),
                  pl.BlockSpec(tri.shape, lambda bi, hp, qi: (0, 0)),
                  pl.BlockSpec(prev.shape, lambda bi, hp, qi: (0, 0)),
                  qblk, seq, seq, qblk],
        out_specs=qblk,
        out_shape=jax.ShapeDtypeStruct((b, s, w), jnp.bfloat16),
        scratch_shapes=[pltpu.VMEM((2, t, PAIR_W), jnp.bfloat16),
                        pltpu.VMEM((2, t, LANES), jnp.float32),
                        pltpu.VMEM((2, t, PAIR_W), jnp.float32),
                        pltpu.VMEM((2, t, LANES), jnp.float32),
                        pltpu.VMEM((2, nc, SCAN_CHUNK), jnp.float32)],
        compiler_params=pltpu.CompilerParams(
            dimension_semantics=("arbitrary", "arbitrary", "arbitrary"),
            vmem_limit_bytes=VMEM_LIMIT),
    )(bound, lf_t, tri, prev, q, k, v, gate)


def _fox_logit_bound(q_gain, k_gain):
    gq = jnp.max(jnp.abs(q_gain), axis=-1)
    gk = jnp.max(jnp.abs(k_gain), axis=-1)
    return (HEAD_DIM * SCALE * 1.01) * gq * gk + 0.05


def _sb_kernel(q_ref, k_ref, v_ref, g_ref, tri_ref, o_ref, qs_ref, r_ref, acc_ref):
    bq, t = SB_BQ, ATT_BK
    sub = bq // t
    qi = pl.program_id(2)
    q0, q1 = _split_heads(q_ref[0])
    qs_ref[0] = q0
    qs_ref[1] = q1
    for h in range(2):
        r_ref[h] = jnp.zeros((bq, LANES), jnp.float32)
        acc_ref[h] = jnp.zeros((bq, PAIR_W), jnp.float32)

    def step(distances):
        tri = tri_ref[...]
        keep = (lax.broadcasted_iota(jnp.int32, (t, t), 1)
                < lax.broadcasted_iota(jnp.int32, (t, t), 0))
        rows = {a: slice(a * t, (a + 1) * t) for a in range(sub)}
        chains, valid, ks, vs = [], {}, {}, {}
        for n, d in enumerate(distances):
            for a in range(sub):
                j = qi * sub + a - d
                valid[n, a] = j >= 0
                start = pl.multiple_of(jnp.maximum(j, 0) * t, t)
                ks[n, a] = k_ref[0, pl.ds(start, t), :]
                vs[n, a] = v_ref[0, pl.ds(start, t), :]
                chains += [(n, a, h) for h in range(2)]
        diagonal = {n: isinstance(d, int) and d == 0 for n, d in enumerate(distances)}
        z = {(n, a, h): _dot_nt(qs_ref[h, rows[a], :], ks[n, a]) for n, a, h in chains}
        sps = {}
        for c in chains:
            sp = _softplus(z[c])
            if diagonal[c[0]]:
                sp = jnp.where(keep, sp, 0.0)
            sps[c] = sp.astype(jnp.bfloat16)
        cum = {c: _dot(sps[c], tri) for c in chains}
        w = {}
        for c in chains:
            wc = jnp.exp(z[c] - cum[c])
            if diagonal[c[0]]:
                wc = jnp.where(keep, wc, 0.0)
            w[c] = wc.astype(jnp.bfloat16)
        pv = {(n, a, h): _dot(w[n, a, h], vs[n, a]) for n, a, h in chains}
        for n, a, h in chains:
            r_prev = r_ref[h, rows[a], :]
            add = jnp.exp(-r_prev) * pv[n, a, h]
            mass = jnp.broadcast_to(cum[n, a, h][:, 0:1], (t, LANES))
            if not diagonal[n]:
                add = jnp.where(valid[n, a], add, 0.0)
                mass = jnp.where(valid[n, a], mass, 0.0)
            acc_ref[h, rows[a], :] = acc_ref[h, rows[a], :] + add
            r_ref[h, rows[a], :] = r_prev + mass

    def finished(d):
        done = jnp.bool_(True)
        for a in range(sub):
            rows = slice(a * t, (a + 1) * t)
            exhausted = qi * sub + a - (d + 1) < 0
            saturated = jnp.min(r_ref[:, rows, :]) >= SB_CUTOFF
            done = jnp.logical_and(done, jnp.logical_or(exhausted, saturated))
        return done

    step([0, 1])

    def cond(carry):
        _, done = carry
        return jnp.logical_not(done)

    def body(carry):
        d, _ = carry
        step([d])
        return d + 1, finished(d)

    lax.while_loop(cond, body, (jnp.int32(2), finished(1)))

    o = jnp.where(_lane_is_first_head((bq, PAIR_W)), acc_ref[0], acc_ref[1])
    o_ref[0] = (o * _silu(g_ref[0])).astype(o_ref.dtype)


def _sb_attention(q, k, v, gate, tri):
    b, s, w = q.shape
    t = SB_BQ
    pairs = w // PAIR_W
    qblk = pl.BlockSpec((1, t, PAIR_W), lambda bi, hp, qi: (bi, qi, hp))
    seq = pl.BlockSpec((1, s, PAIR_W), lambda bi, hp, qi: (bi, 0, hp))
    return pl.pallas_call(
        _sb_kernel,
        grid=(b, pairs, s // t),
        in_specs=[qblk, seq, seq, qblk,
                  pl.BlockSpec(tri.shape, lambda bi, hp, qi: (0, 0))],
        out_specs=qblk,
        out_shape=jax.ShapeDtypeStruct((b, s, w), jnp.bfloat16),
        scratch_shapes=[pltpu.VMEM((2, t, PAIR_W), jnp.bfloat16),
                        pltpu.VMEM((2, t, LANES), jnp.float32),
                        pltpu.VMEM((2, t, PAIR_W), jnp.float32)],
        compiler_params=pltpu.CompilerParams(
            dimension_semantics=("arbitrary", "arbitrary", "arbitrary"),
            vmem_limit_bytes=VMEM_LIMIT),
    )(q, k, v, gate, tri)


def _out_proj_kernel(x_ref, fox_ref, sb_ref, px_ref, hist_ref, pg_ref,
                     wpool_ref, pscale_ref, wout_ref, o_ref, xp_ref, wo_ref, *, blocks_per_seq):
    bm = ROW_BLOCK
    i = pl.program_id(0)

    @pl.when(i == 0)
    def _():
        chunk = D_MIX // WEIGHT_PACK_CHUNKS
        for r in range(0, D_MIX, chunk):
            wo_ref[r:r + chunk, :] = wout_ref[r:r + chunk, :].astype(jnp.bfloat16)

    pos0 = (i % blocks_per_seq) * bm
    hist = hist_ref[...]
    xp_ref[0:MAX_WINDOW, :] = jnp.where(pos0 == 0, jnp.zeros_like(hist), hist)
    px = px_ref[...]
    xp_ref[MAX_WINDOW:MAX_WINDOW + bm, :] = px

    sums = {}
    run = px
    for d in range(1, MAX_WINDOW):
        run = run + xp_ref[MAX_WINDOW - d:MAX_WINDOW - d + bm, :]
        if d + 1 in POOL_WINDOWS:
            sums[d + 1] = run
    group = lax.broadcasted_iota(jnp.int32, (bm, POOL_W), 1) // POOL_GROUP_DIM
    pos = pos0 + lax.broadcasted_iota(jnp.int32, (bm, POOL_W), 0)
    wsum = sums[POOL_WINDOWS[-1]]
    win = jnp.full((bm, POOL_W), POOL_WINDOWS[-1], jnp.int32)
    for g in range(POOL_GROUPS - 2, -1, -1):
        wsum = jnp.where(group == g, sums[POOL_WINDOWS[g]], wsum)
        win = jnp.where(group == g, POOL_WINDOWS[g], win)
    count = jnp.minimum(pos + 1, win).astype(jnp.float32)
    pooled = wsum / count - px
    y = _dot(pooled.astype(jnp.bfloat16), wpool_ref[...]) * pscale_ref[...]
    pool_out = (y * _silu(pg_ref[...])).astype(jnp.bfloat16)

    o_ref[...] = (x_ref[...]
                  + _dot(fox_ref[...], wo_ref[0:FOX_W, :])
                  + _dot(pool_out, wo_ref[FOX_W:FOX_W + POOL_W, :])
                  + _dot(sb_ref[...], wo_ref[FOX_W + POOL_W:D_MIX, :]))


def _out_proj(layer, x2d, fox_o, sb_o, px, pg, wpool_bd, pscale, w_out, seq_len):
    m = x2d.shape[0]
    bm = ROW_BLOCK
    row = lambda w: pl.BlockSpec((bm, w), lambda i: (i, 0))
    per_layer = lambda a: _layer_spec(a, layer)
    hist_per_block = bm // MAX_WINDOW
    hist = pl.BlockSpec((MAX_WINDOW, POOL_W),
                        lambda i: (jnp.maximum(i * hist_per_block - 1, 0), 0))
    kern = functools.partial(_out_proj_kernel, blocks_per_seq=seq_len // bm)
    return pl.pallas_call(
        kern,
        grid=(m // bm,),
        in_specs=[row(D_MODEL), row(FOX_W), row(SB_W), row(POOL_W), hist, row(POOL_W),
                  per_layer(wpool_bd), per_layer(pscale),
                  _layer_spec(w_out, layer, single_buffer=True)],
        out_specs=row(D_MODEL),
        out_shape=jax.ShapeDtypeStruct((m, D_MODEL), jnp.float32),
        scratch_shapes=[pltpu.VMEM((MAX_WINDOW + bm, POOL_W), jnp.float32),
                        pltpu.VMEM((D_MIX, D_MODEL), jnp.bfloat16)],
        compiler_params=pltpu.CompilerParams(
            dimension_semantics=("arbitrary",), vmem_limit_bytes=VMEM_LIMIT),
    )(x2d, fox_o, sb_o, px, px, pg, wpool_bd, pscale, w_out)


def _constants(seq_len):
    bf16 = jnp.bfloat16
    idx = jnp.arange(MXU_TILE)
    hsum =(idx[:, None] // HEAD_DIM == idx[None, :] // HEAD_DIM).astype(bf16)
    c = jnp.arange(SCAN_CHUNK)
    scan_tri = (c[:, None] <= c[None, :]).astype(bf16)
    nc = seq_len // SCAN_CHUNK
    r = jnp.arange(nc)
    scan_prev = (r[None, :] < r[:, None]).astype(bf16)
    a = jnp.arange(ATT_BK)
    sb_tri = (a[:, None] >= a[None, :]).astype(bf16)
    return hsum, scan_tri, scan_prev, sb_tri


def _block_diag(w_pool):
    out = jnp.zeros((w_pool.shape[0], POOL_W, POOL_W), w_pool.dtype)
    for g in range(POOL_GROUPS):
        lo = g * POOL_GROUP_DIM
        out = out.at[:, lo:lo + POOL_GROUP_DIM, lo:lo + POOL_GROUP_DIM].set(w_pool[:, g])
    return out.astype(jnp.bfloat16)


def kernel(x, norm_g, w_in, b_f, q_norm_g, k_norm_g, w_pool, pool_scale, w_out):
    b, s, d = x.shape
    depth = norm_g.shape[0]
    assert d == D_MODEL and s % ROW_BLOCK == 0
    assert s % ATT_BQ == 0 and ATT_BQ % ATT_BK == 0 and s % SB_BQ == 0 and SB_BQ % ATT_BK == 0
    assert w_in.shape[-1] == C_FF + FOX_HEADS
    hsum, scan_tri, scan_prev, sb_tri = _constants(s)
    nc = s // SCAN_CHUNK
    x2d = x.reshape(b * s, d)
    wpool_bd = _block_diag(w_pool)
    w_in_bf = w_in.astype(jnp.bfloat16)
    rowvec = lambda a: a[:, None, :]
    norm_g3, pscale3 = rowvec(norm_g), rowvec(pool_scale)
    b_f3 = rowvec(jnp.pad(b_f, ((0, 0), (0, FF_PAD - FOX_HEADS))))
    q_g = rowvec(jnp.tile(q_norm_g, (1, FOX_HEADS)))
    k_g = rowvec(jnp.tile(k_norm_g, (1, FOX_HEADS)))
    bounds = _fox_logit_bound(q_norm_g, k_norm_g)[:, None]
    for l in range(depth):
        (fq, fk, fv, fg, px, pg, sq, sk, sv, sg, lf) = _in_proj(
            l, x2d, norm_g3, w_in_bf, hsum, q_g, k_g, b_f3)

        lf_t = lf.reshape(FOX_HEADS, b * nc, SCAN_CHUNK)

        r3 = lambda a: a.reshape(b, s, a.shape[-1])
        fox_o = _fox_attention(bounds[l:l + 1], lf_t, scan_tri, scan_prev,
                               r3(fq), r3(fk), r3(fv), r3(fg))
        sb_o = _sb_attention(r3(sq), r3(sk), r3(sv), r3(sg), sb_tri)

        x2d = _out_proj(l, x2d, fox_o.reshape(b * s, FOX_W), sb_o.reshape(b * s, SB_W),
                        px, pg, wpool_bd, pscale3, w_out, s)
    return x2d.reshape(b, s, d)
```

```python
import functools

import jax
import jax.numpy as jnp
from jax import lax
from jax.experimental import pallas as pl
from jax.experimental.pallas import tpu as pltpu

D_MODEL = 1024
HEAD_DIM = 64
FOX_HEADS = 8
SB_HEADS = 4
POOL_GROUPS = 4
POOL_WINDOWS = (2, 4, 8, 16)
POOL_GROUP_DIM = 64
FOX_W = FOX_HEADS * HEAD_DIM
SB_W = SB_HEADS * HEAD_DIM
POOL_W = POOL_GROUPS * POOL_GROUP_DIM
D_MIX = FOX_W + POOL_W + SB_W
EPS = 1e-6
NEG = -1e30
SCALE = HEAD_DIM ** -0.5
LOG2E = 1.4426950408889634
FOX_MAX_BOUND = 30.0
SB_CUTOFF = 106.0
FOX_CUTOFF = 106.0

LANES = 128
MXU_TILE = 256
PAIR_W = 2 * HEAD_DIM
MAX_WINDOW = max(POOL_WINDOWS)
FF_PAD = LANES

ROW_BLOCK = 1024
WEIGHT_PACK_CHUNKS = 8
ATT_BQ = 1024
SB_BQ = 2048
ATT_BK = 256
FOX_BK_FULL = 512
FOX_BLOCKS_PER_TRIP = 2
SCAN_CHUNK = LANES

VMEM_LIMIT = 56 * 1024 * 1024

C_FQ, C_FK, C_FV, C_FG = 0, FOX_W, 2 * FOX_W, 3 * FOX_W
C_PX = 4 * FOX_W
C_PG = C_PX + POOL_W
C_SQ = C_PG + POOL_W
C_SK = C_SQ + SB_W
C_SV = C_SK + SB_W
C_SG = C_SV + SB_W
C_FF = C_SG + SB_W
D_IN_PAD = C_FF + FF_PAD


def _bf16_split3(x):
    a = x.astype(jnp.bfloat16)
    r = x - a.astype(jnp.float32)
    b = r.astype(jnp.bfloat16)
    c = (r - b.astype(jnp.float32)).astype(jnp.bfloat16)
    return a, b, c


def _dot(a, b):
    return jnp.dot(a, b, preferred_element_type=jnp.float32)


def _dot_nt(a, b):
    return lax.dot_general(a, b, (((1,), (1,)), ((), ())),
                           preferred_element_type=jnp.float32)


def _silu(x):
    return x * (1.0 / (1.0 + jnp.exp(-x)))


def _softplus(x):
    return jnp.maximum(x, 0.0) + jnp.log(1.0 + jnp.exp2(jnp.abs(x) * (-LOG2E)))


def _in_proj_kernel(x_ref, g_ref, win_ref, hsum_ref, qg_ref, kg_ref, bf_ref,
                    fq_ref, fk_ref, fv_ref, fg_ref, px_ref, pg_ref,
                    sq_ref, sk_ref, sv_ref, sg_ref, lf_ref, w_ref):
    @pl.when(pl.program_id(0) == 0)
    def _():
        chunk = D_MODEL // WEIGHT_PACK_CHUNKS
        for r in range(0, D_MODEL, chunk):
            rows = slice(r, r + chunk)
            w_ref[rows, 0:C_PX] = win_ref[rows, 0:C_PX].astype(jnp.bfloat16)
            w_ref[rows, C_PX:C_FF] = win_ref[rows, C_PX + FOX_HEADS:].astype(jnp.bfloat16)
            ff = win_ref[rows, C_PX:C_PX + FOX_HEADS].astype(jnp.bfloat16)
            w_ref[rows, C_FF:] = jnp.concatenate(
                [ff, jnp.zeros((chunk, FF_PAD - FOX_HEADS), jnp.bfloat16)], axis=1)

    x = x_ref[...]
    ms = jnp.mean(x * x, axis=-1, keepdims=True)
    h = (x * lax.rsqrt(ms + EPS) * g_ref[...]).astype(jnp.bfloat16)

    def proj(c0, width):
        return _dot(h, w_ref[:, c0:c0 + width])

    def head_norm(y, gain):
        sq = (y * y).astype(jnp.bfloat16)
        hsum = hsum_ref[...]
        w = hsum.shape[0]
        ssq = jnp.concatenate(
            [_dot(sq[:, c:c + w], hsum) for c in range(0, y.shape[1], w)], axis=1)
        return y * lax.rsqrt(ssq * (1.0 / HEAD_DIM) + EPS) * gain

    fq_ref[...] = (head_norm(proj(C_FQ, FOX_W), qg_ref[...]) * SCALE).astype(jnp.bfloat16)
    fk_ref[...] = head_norm(proj(C_FK, FOX_W), kg_ref[...]).astype(jnp.bfloat16)
    fv_ref[...] = proj(C_FV, FOX_W).astype(jnp.bfloat16)
    fg_ref[...] = proj(C_FG, FOX_W)
    px_ref[...] = proj(C_PX, POOL_W)
    pg_ref[...] = proj(C_PG, POOL_W)
    sq_ref[...] = (proj(C_SQ, SB_W) * SCALE).astype(jnp.bfloat16)
    sk_ref[...] = proj(C_SK, SB_W).astype(jnp.bfloat16)
    sv_ref[...] = proj(C_SV, SB_W).astype(jnp.bfloat16)
    sg_ref[...] = proj(C_SG, SB_W)
    ff = proj(C_FF, FF_PAD) + bf_ref[...]
    lf_ref[...] = (-_softplus(-ff)).T[:FOX_HEADS, :]


def _layer_spec(stacked, layer, single_buffer=False):
    tail = stacked.shape[1:]
    mode = dict(pipeline_mode=pl.Buffered(1)) if single_buffer else {}
    return pl.BlockSpec((None,) + tail, lambda i: (layer,) + (0,) * len(tail), **mode)


def _in_proj(layer, x2d, norm_g, w_in, hsum, q_g, k_g, b_f):
    m = x2d.shape[0]
    bm = ROW_BLOCK
    row = lambda w: pl.BlockSpec((bm, w), lambda i: (i, 0))
    full = lambda a: pl.BlockSpec(a.shape, lambda i: (0,) * a.ndim)
    per_layer = lambda a: _layer_spec(a, layer)
    bf16, f32 = jnp.bfloat16, jnp.float32
    outs = [(FOX_W, bf16), (FOX_W, bf16), (FOX_W, bf16), (FOX_W, f32),
            (POOL_W, f32), (POOL_W, f32),
            (SB_W, bf16), (SB_W, bf16), (SB_W, bf16), (SB_W, f32)]
    return pl.pallas_call(
        _in_proj_kernel,
        grid=(m // bm,),
        in_specs=[row(D_MODEL), per_layer(norm_g), _layer_spec(w_in, layer, single_buffer=True),
                  full(hsum), per_layer(q_g), per_layer(k_g), per_layer(b_f)],
        out_specs=[row(w) for w, _ in outs] + [pl.BlockSpec((FOX_HEADS, bm), lambda i: (0, i))],
        out_shape=([jax.ShapeDtypeStruct((m, w), dt) for w, dt in outs]
                   + [jax.ShapeDtypeStruct((FOX_HEADS, m), f32)]),
        scratch_shapes=[pltpu.VMEM((D_MODEL, D_IN_PAD), jnp.bfloat16)],
        compiler_params=pltpu.CompilerParams(
            dimension_semantics=("arbitrary",), vmem_limit_bytes=VMEM_LIMIT),
    )(x2d, norm_g, w_in, hsum, q_g, k_g, b_f)


def _lane_is_first_head(shape):
    return lax.broadcasted_iota(jnp.int32, shape, len(shape) - 1) < HEAD_DIM


def _split_heads(q):
    first = _lane_is_first_head(q.shape)
    zero = jnp.zeros_like(q)
    return jnp.where(first, q, zero), jnp.where(first, zero, q)


def _widen(col, width):
    return jnp.concatenate([col] * (width // LANES), axis=1)


def _fox_kernel(bound_ref, lf_ref, tri_ref, prev_ref, *refs):
    c_ref = refs[-1]

    @pl.when(jnp.logical_and(pl.program_id(1) == 0, pl.program_id(2) == 0))
    def _():
        lf = lf_ref[...].reshape(c_ref.shape)
        within = sum(_dot(p, tri_ref[...]) for p in _bf16_split3(lf))
        total = jnp.broadcast_to(within[:, SCAN_CHUNK - 1:SCAN_CHUNK], within.shape)
        c_ref[...] = within + sum(_dot(prev_ref[...], p) for p in _bf16_split3(total))

    bounded = bound_ref[0, 0] <= FOX_MAX_BOUND

    @pl.when(bounded)
    def _():
        _fox_path(bound_ref, *refs, running_max=False)

    @pl.when(jnp.logical_not(bounded))
    def _():
        _fox_path(bound_ref, *refs, running_max=True)


def _fox_path(bound_ref, q_ref, k_ref, v_ref, g_ref, o_ref,
              qs_ref, m_ref, acc_ref, ct_ref, c_ref, *, running_max):
    bq, bk = ATT_BQ, ATT_BK
    qi = pl.program_id(2)
    q0, q1 = _split_heads(q_ref[0])
    qs_ref[0] = q0
    qs_ref[1] = q1
    shift = 0.0 if running_max else bound_ref[0, 0]
    chunks = bq // SCAN_CHUNK
    nc = c_ref.shape[0] // FOX_HEADS
    base = [pl.multiple_of((2 * pl.program_id(1) + h) * nc, nc) for h in range(2)]
    c_first = []
    for h in range(2):
        m_ref[h] = jnp.full((bq, LANES), NEG, jnp.float32)
        acc_ref[h] = jnp.zeros((bq, PAIR_W), jnp.float32)
        mine = c_ref[pl.ds(pl.multiple_of(base[h] + qi * chunks, chunks), chunks), :]
        cols = jnp.tile(mine, (LANES // chunks, 1)).T
        c_first.append(cols[0:1, 0:1])
        for n in range(chunks):
            ct_ref[h, n * SCAN_CHUNK:(n + 1) * SCAN_CHUNK, :] = (
                jnp.broadcast_to(cols[:, n:n + 1], (SCAN_CHUNK, LANES)) - shift)

    def run(blocks, masked, heads=(0, 1)):
        chains = []
        for j, bk, r0 in blocks:
            n = bq - r0
            start = pl.multiple_of(j * bk, bk)
            k = k_ref[0, pl.ds(start, bk), :]
            v = v_ref[0, pl.ds(start, bk), :]
            first = _lane_is_first_head(v.shape)
            ones = jnp.ones_like(v)
            vs = (jnp.where(first, v, ones), jnp.where(first, ones, v))
            keep = None
            if masked:
                keep = (lax.broadcasted_iota(jnp.int32, (n, bk), 1)
                        <= lax.broadcasted_iota(jnp.int32, (n, bk), 0))
            for h in heads:
                first_chunk = pl.multiple_of(j * (bk // SCAN_CHUNK), bk // SCAN_CHUNK)
                cs = jnp.concatenate([c_ref[pl.ds(base[h] + first_chunk + n, 1), :]
                                      for n in range(bk // SCAN_CHUNK)], axis=1)
                chains.append((h, slice(r0, bq), bk, k, vs[h], cs, keep))
        qk = [_dot_nt(qs_ref[h, rows, :], k) for h, rows, _, k, _, _, _ in chains]
        ps = []
        for (h, rows, bk, _, _, cs, keep), s in zip(chains, qk):
            s = s + (_widen(ct_ref[h, rows, :], bk) - cs)
            if masked:
                s = jnp.where(keep, s, NEG)
            if running_max:
                m_prev = m_ref[h, rows, :]
                m_next = jnp.maximum(m_prev, jnp.max(s, axis=1, keepdims=True))
                s = s - _widen(m_next, bk)
                acc_ref[h, rows, :] = jnp.exp(m_prev - m_next) * acc_ref[h, rows, :]
                m_ref[h, rows, :] = m_next
            ps.append(jnp.exp(s).astype(jnp.bfloat16))
        for (h, rows, _, _, v, _, _), p in zip(chains, ps):
            acc_ref[h, rows, :] = acc_ref[h, rows, :] + _dot(p, v)

    sub = bq // bk
    diagonal = [(qi * sub + jj, bk, jj * bk) for jj in reversed(range(sub))]
    if running_max:
        for blk in diagonal:
            run([blk], True)
    else:
        run(diagonal, True)

    n_blocks = qi * (bq // FOX_BK_FULL)

    def span(first, count, n, heads):
        def body(j, carry):
            run([(first + j * n + u, FOX_BK_FULL, 0) for u in range(n)], False, heads)
            return carry
        lax.fori_loop(0, count, body, 0)

    if running_max:
        span(0, n_blocks, 1, (0, 1))
    else:
        pos = (lax.broadcasted_iota(jnp.int32, (nc, SCAN_CHUNK), 0) * SCAN_CHUNK
               + lax.broadcasted_iota(jnp.int32, (nc, SCAN_CHUNK), 1))
        live_from = []
        for h in range(2):
            bias = c_first[h] - c_ref[pl.ds(base[h], nc), :]
            dead = jnp.logical_and(bias <= -FOX_CUTOFF, pos < qi * bq)
            n_dead = jnp.sum(jnp.where(dead, 1.0, 0.0)).astype(jnp.int32)
            live_from.append(n_dead // FOX_BK_FULL)
        both_from = jnp.maximum(live_from[0], live_from[1])
        pair = FOX_BLOCKS_PER_TRIP
        for h in range(2):
            alone = both_from - live_from[h]
            span(live_from[h], alone % pair, 1, (h,))
            span(live_from[h] + alone % pair, alone // pair, pair, (h,))
        shared = n_blocks - both_from
        span(both_from, shared % pair, 1, (0, 1))
        start = both_from + shared % pair
        span(start, (shared // pair) % 2, pair, (0, 1))
        span(start + ((shared // pair) % 2) * pair, shared // (2 * pair), 2 * pair, (0, 1))

    a0, a1 = acc_ref[0], acc_ref[1]
    first = _lane_is_first_head(a0.shape)
    values = jnp.where(first, a0, a1)
    sums = pltpu.roll(jnp.where(first, a1, a0), HEAD_DIM, axis=1)
    o_ref[0] = (values / sums * _silu(g_ref[0])).astype(o_ref.dtype)


def _fox_attention(bound, lf_t, tri, prev, q, k, v, gate):
    b, s, w = q.shape
    t = ATT_BQ
    pairs = w // PAIR_W
    nc = s // SCAN_CHUNK
    qblk = pl.BlockSpec((1, t, PAIR_W), lambda bi, hp, qi: (bi, qi, hp))
    seq = pl.BlockSpec((1, s, PAIR_W), lambda bi, hp, qi: (bi, 0, hp))
    return pl.pallas_call(
        _fox_kernel,
        grid=(b, pairs, s // t),
        in_specs=[pl.BlockSpec(memory_space=pltpu.SMEM),
                  pl.BlockSpec((lf_t.shape[0], nc, SCAN_CHUNK), lambda bi, hp, qi: (0, bi, 0)),
                  pl.BlockSpec(tri.shape, lambda bi, hp, qi: (0, 0)),
                  pl.BlockSpec(prev.shape, lambda bi, hp, qi: (0, 0)),
                  qblk, seq, seq, qblk],
        out_specs=qblk,
        out_shape=jax.ShapeDtypeStruct((b, s, w), jnp.bfloat16),
        scratch_shapes=[pltpu.VMEM((2, t, PAIR_W), jnp.bfloat16),
                        pltpu.VMEM((2, t, LANES), jnp.float32),
                        pltpu.VMEM((2, t, PAIR_W), jnp.float32),
                        pltpu.VMEM((2, t, LANES), jnp.float32),
                        pltpu.VMEM((lf_t.shape[0] * nc, SCAN_CHUNK), jnp.float32)],
        compiler_params=pltpu.CompilerParams(
            dimension_semantics=("arbitrary", "arbitrary", "arbitrary"),
            vmem_limit_bytes=VMEM_LIMIT),
    )(bound, lf_t, tri, prev, q, k, v, gate)


def _fox_logit_bound(q_gain, k_gain):
    gq = jnp.max(jnp.abs(q_gain), axis=-1)
    gk = jnp.max(jnp.abs(k_gain), axis=-1)
    return (HEAD_DIM * SCALE * 1.01) * gq * gk + 0.05


def _sb_kernel(q_ref, k_ref, v_ref, g_ref, tri_ref, o_ref, qs_ref, r_ref, acc_ref):
    bq, t = SB_BQ, ATT_BK
    sub = bq // t
    qi = pl.program_id(2)
    q0, q1 = _split_heads(q_ref[0])
    qs_ref[0] = q0
    qs_ref[1] = q1
    for h in range(2):
        r_ref[h] = jnp.zeros((bq, LANES), jnp.float32)
        acc_ref[h] = jnp.zeros((bq, PAIR_W), jnp.float32)

    def step(distances):
        tri = tri_ref[...]
        keep = (lax.broadcasted_iota(jnp.int32, (t, t), 1)
                < lax.broadcasted_iota(jnp.int32, (t, t), 0))
        rows = {a: slice(a * t, (a + 1) * t) for a in range(sub)}
        chains, valid, ks, vs = [], {}, {}, {}
        for n, d in enumerate(distances):
            for a in range(sub):
                j = qi * sub + a - d
                valid[n, a] = j >= 0
                start = pl.multiple_of(jnp.maximum(j, 0) * t, t)
                ks[n, a] = k_ref[0, pl.ds(start, t), :]
                vs[n, a] = v_ref[0, pl.ds(start, t), :]
                chains += [(n, a, h) for h in range(2)]
        diagonal = {n: isinstance(d, int) and d == 0 for n, d in enumerate(distances)}
        z = {(n, a, h): _dot_nt(qs_ref[h, rows[a], :], ks[n, a]) for n, a, h in chains}
        sps = {}
        for c in chains:
            sp = _softplus(z[c])
            if diagonal[c[0]]:
                sp = jnp.where(keep, sp, 0.0)
            sps[c] = sp.astype(jnp.bfloat16)
        cum = {c: _dot(sps[c], tri) for c in chains}
        w = {}
        for c in chains:
            wc = jnp.exp(z[c] - cum[c])
            if diagonal[c[0]]:
                wc = jnp.where(keep, wc, 0.0)
            w[c] = wc.astype(jnp.bfloat16)
        pv = {(n, a, h): _dot(w[n, a, h], vs[n, a]) for n, a, h in chains}
        for n, a, h in chains:
            r_prev = r_ref[h, rows[a], :]
            add = jnp.exp(-r_prev) * pv[n, a, h]
            mass = jnp.broadcast_to(cum[n, a, h][:, 0:1], (t, LANES))
            if not diagonal[n]:
                add = jnp.where(valid[n, a], add, 0.0)
                mass = jnp.where(valid[n, a], mass, 0.0)
            acc_ref[h, rows[a], :] = acc_ref[h, rows[a], :] + add
            r_ref[h, rows[a], :] = r_prev + mass

    def finished(d):
        done = jnp.bool_(True)
        for a in range(sub):
            rows = slice(a * t, (a + 1) * t)
            exhausted = qi * sub + a - (d + 1) < 0
            saturated = jnp.min(r_ref[:, rows, :]) >= SB_CUTOFF
            done = jnp.logical_and(done, jnp.logical_or(exhausted, saturated))
        return done

    step([0, 1])

    def cond(carry):
        _, done = carry
        return jnp.logical_not(done)

    def body(carry):
        d, _ = carry
        step([d])
        return d + 1, finished(d)

    lax.while_loop(cond, body, (jnp.int32(2), finished(1)))

    o = jnp.where(_lane_is_first_head((bq, PAIR_W)), acc_ref[0], acc_ref[1])
    o_ref[0] = (o * _silu(g_ref[0])).astype(o_ref.dtype)


def _sb_attention(q, k, v, gate, tri):
    b, s, w = q.shape
    t = SB_BQ
    pairs = w // PAIR_W
    qblk = pl.BlockSpec((1, t, PAIR_W), lambda bi, hp, qi: (bi, qi, hp))
    seq = pl.BlockSpec((1, s, PAIR_W), lambda bi, hp, qi: (bi, 0, hp))
    return pl.pallas_call(
        _sb_kernel,
        grid=(b, pairs, s // t),
        in_specs=[qblk, seq, seq, qblk,
                  pl.BlockSpec(tri.shape, lambda bi, hp, qi: (0, 0))],
        out_specs=qblk,
        out_shape=jax.ShapeDtypeStruct((b, s, w), jnp.bfloat16),
        scratch_shapes=[pltpu.VMEM((2, t, PAIR_W), jnp.bfloat16),
                        pltpu.VMEM((2, t, LANES), jnp.float32),
                        pltpu.VMEM((2, t, PAIR_W), jnp.float32)],
        compiler_params=pltpu.CompilerParams(
            dimension_semantics=("arbitrary", "arbitrary", "arbitrary"),
            vmem_limit_bytes=VMEM_LIMIT),
    )(q, k, v, gate, tri)


def _out_proj_kernel(x_ref, fox_ref, sb_ref, px_ref, hist_ref, pg_ref,
                     wpool_ref, pscale_ref, wout_ref, o_ref, xp_ref, wo_ref, *, blocks_per_seq):
    bm = ROW_BLOCK
    i = pl.program_id(0)

    @pl.when(i == 0)
    def _():
        chunk = D_MIX // WEIGHT_PACK_CHUNKS
        for r in range(0, D_MIX, chunk):
            wo_ref[r:r + chunk, :] = wout_ref[r:r + chunk, :].astype(jnp.bfloat16)

    pos0 = (i % blocks_per_seq) * bm
    hist = hist_ref[...]
    xp_ref[0:MAX_WINDOW, :] = jnp.where(pos0 == 0, jnp.zeros_like(hist), hist)
    px = px_ref[...]
    xp_ref[MAX_WINDOW:MAX_WINDOW + bm, :] = px

    sums = {}
    run = px
    for d in range(1, MAX_WINDOW):
        run = run + xp_ref[MAX_WINDOW - d:MAX_WINDOW - d + bm, :]
        if d + 1 in POOL_WINDOWS:
            sums[d + 1] = run
    group = lax.broadcasted_iota(jnp.int32, (bm, POOL_W), 1) // POOL_GROUP_DIM
    pos = pos0 + lax.broadcasted_iota(jnp.int32, (bm, POOL_W), 0)
    wsum = sums[POOL_WINDOWS[-1]]
    win = jnp.full((bm, POOL_W), POOL_WINDOWS[-1], jnp.int32)
    for g in range(POOL_GROUPS - 2, -1, -1):
        wsum = jnp.where(group == g, sums[POOL_WINDOWS[g]], wsum)
        win = jnp.where(group == g, POOL_WINDOWS[g], win)
    count = jnp.minimum(pos + 1, win).astype(jnp.float32)
    pooled = wsum / count - px
    y = _dot(pooled.astype(jnp.bfloat16), wpool_ref[...]) * pscale_ref[...]
    pool_out = (y * _silu(pg_ref[...])).astype(jnp.bfloat16)

    o_ref[...] = (x_ref[...]
                  + _dot(fox_ref[...], wo_ref[0:FOX_W, :])
                  + _dot(pool_out, wo_ref[FOX_W:FOX_W + POOL_W, :])
                  + _dot(sb_ref[...], wo_ref[FOX_W + POOL_W:D_MIX, :]))


def _out_proj(layer, x2d, fox_o, sb_o, px, pg, wpool_bd, pscale, w_out, seq_len):
    m = x2d.shape[0]
    bm = ROW_BLOCK
    row = lambda w: pl.BlockSpec((bm, w), lambda i: (i, 0))
    per_layer = lambda a: _layer_spec(a, layer)
    hist_per_block = bm // MAX_WINDOW
    hist = pl.BlockSpec((MAX_WINDOW, POOL_W),
                        lambda i: (jnp.maximum(i * hist_per_block - 1, 0), 0))
    kern = functools.partial(_out_proj_kernel, blocks_per_seq=seq_len // bm)
    return pl.pallas_call(
        kern,
        grid=(m // bm,),
        in_specs=[row(D_MODEL), row(FOX_W), row(SB_W), row(POOL_W), hist, row(POOL_W),
                  per_layer(wpool_bd), per_layer(pscale),
                  _layer_spec(w_out, layer, single_buffer=True)],
        out_specs=row(D_MODEL),
        out_shape=jax.ShapeDtypeStruct((m, D_MODEL), jnp.float32),
        scratch_shapes=[pltpu.VMEM((MAX_WINDOW + bm, POOL_W), jnp.float32),
                        pltpu.VMEM((D_MIX, D_MODEL), jnp.bfloat16)],
        compiler_params=pltpu.CompilerParams(
            dimension_semantics=("arbitrary",), vmem_limit_bytes=VMEM_LIMIT),
    )(x2d, fox_o, sb_o, px, px, pg, wpool_bd, pscale, w_out)


def _constants(seq_len):
    bf16 = jnp.bfloat16
    idx = jnp.arange(MXU_TILE)
    hsum =(idx[:, None] // HEAD_DIM == idx[None, :] // HEAD_DIM).astype(bf16)
    c = jnp.arange(SCAN_CHUNK)
    scan_tri = (c[:, None] <= c[None, :]).astype(bf16)
    nc = seq_len // SCAN_CHUNK
    r = jnp.arange(FOX_HEADS * nc)
    scan_prev = ((r[:, None] // nc == r[None, :] // nc)
                 & (r[None, :] < r[:, None])).astype(bf16)
    a = jnp.arange(ATT_BK)
    sb_tri = (a[:, None] >= a[None, :]).astype(bf16)
    return hsum, scan_tri, scan_prev, sb_tri


def _block_diag(w_pool):
    out = jnp.zeros((w_pool.shape[0], POOL_W, POOL_W), w_pool.dtype)
    for g in range(POOL_GROUPS):
        lo = g * POOL_GROUP_DIM
        out = out.at[:, lo:lo + POOL_GROUP_DIM, lo:lo + POOL_GROUP_DIM].set(w_pool[:, g])
    return out.astype(jnp.bfloat16)


def kernel(x, norm_g, w_in, b_f, q_norm_g, k_norm_g, w_pool, pool_scale, w_out):
    b, s, d = x.shape
    depth = norm_g.shape[0]
    assert d == D_MODEL and s % ROW_BLOCK == 0
    assert s % ATT_BQ == 0 and ATT_BQ % ATT_BK == 0 and s % SB_BQ == 0 and SB_BQ % ATT_BK == 0
    assert w_in.shape[-1] == C_FF + FOX_HEADS
    hsum, scan_tri, scan_prev, sb_tri = _constants(s)
    nc = s // SCAN_CHUNK
    x2d = x.reshape(b * s, d)
    wpool_bd = _block_diag(w_pool)
    w_in_bf = w_in.astype(jnp.bfloat16)
    rowvec = lambda a: a[:, None, :]
    norm_g3, pscale3 = rowvec(norm_g), rowvec(pool_scale)
    b_f3 = rowvec(jnp.pad(b_f, ((0, 0), (0, FF_PAD - FOX_HEADS))))
    q_g = rowvec(jnp.tile(q_norm_g, (1, FOX_HEADS)))
    k_g = rowvec(jnp.tile(k_norm_g, (1, FOX_HEADS)))
    bounds = _fox_logit_bound(q_norm_g, k_norm_g)[:, None]
    for l in range(depth):
        (fq, fk, fv, fg, px, pg, sq, sk, sv, sg, lf) = _in_proj(
            l, x2d, norm_g3, w_in_bf, hsum, q_g, k_g, b_f3)

        lf_t = lf.reshape(FOX_HEADS, b * nc, SCAN_CHUNK)

        r3 = lambda a: a.reshape(b, s, a.shape[-1])
        fox_o = _fox_attention(bounds[l:l + 1], lf_t, scan_tri, scan_prev,
                               r3(fq), r3(fk), r3(fv), r3(fg))
        sb_o = _sb_attention(r3(sq), r3(sk), r3(sv), r3(sg), sb_tri)

        x2d = _out_proj(l, x2d, fox_o.reshape(b * s, FOX_W), sb_o.reshape(b * s, SB_W),
                        px, pg, wpool_bd, pscale3, w_out, s)
    return x2d.reshape(b, s, d)
```
